```python
import jax, jax.numpy as jnp
from jax import lax
import numpy as np

D_MODEL = 1024
BATCH = 4
SEQ = 8192
DEPTH = 4

CHUNK = 64
D_MIX = D_MODEL
EPS = 1e-6

LRU_WIDTH = 3 * D_MIX // 8
LRU_HEAD_DIM = 64
LRU_HEADS = LRU_WIDTH // LRU_HEAD_DIM
CONV_WIDTH = 4
LRU_C = 8.0

MLA_HEADS = 6
MLA_NOPE = 64
MLA_ROPE = 32
MLA_V = 64
MLA_WIDTH = MLA_HEADS * MLA_V
Q_RANK = 192
KV_RANK = 128
ROPE_THETA = 10000.0
Q_BLOCK = 128

SGU_WIDTH = D_MIX - LRU_WIDTH - MLA_WIDTH
SGU_GROUPS = 4
SGU_GROUP_DIM = SGU_WIDTH // SGU_GROUPS
SGU_BLOCK = 128

IN_SPLITS = (LRU_WIDTH, LRU_WIDTH,
             Q_RANK, KV_RANK, MLA_ROPE, MLA_WIDTH,
             SGU_WIDTH, SGU_WIDTH, SGU_WIDTH)
D_IN = sum(IN_SPLITS)

kernel_name = "hybrid_rglru_mla_sgu_sandwich"


def rms_norm(x, g):
    xf = x.astype(jnp.float32)
    y = xf * lax.rsqrt(jnp.mean(xf * xf, axis=-1, keepdims=True) + EPS)
    return (y * g.astype(jnp.float32)).astype(x.dtype)


def layer_norm(x, g, b):
    xf = x.astype(jnp.float32)
    mu = jnp.mean(xf, axis=-1, keepdims=True)
    xc = xf - mu
    y = xc * lax.rsqrt(jnp.mean(xc * xc, axis=-1, keepdims=True) + EPS)
    return (y * g.astype(jnp.float32) + b.astype(jnp.float32)).astype(x.dtype)


def causal_depthwise_conv(x, w, b):
    y = lax.conv_general_dilated(
        x, w[:, None, :].astype(x.dtype), window_strides=(1,),
        padding=[(CONV_WIDTH - 1, 0)],
        dimension_numbers=('NWC', 'WIO', 'NWC'),
        feature_group_count=x.shape[-1])
    return y + b


def rg_lru(x, wa, ba, wx, bx, lam):
    B_, S_, _ = x.shape
    xh = x.reshape(B_, S_, LRU_HEADS, LRU_HEAD_DIM)
    gate_a = jax.nn.sigmoid(jnp.einsum('bshi,hij->bshj', xh, wa).reshape(B_, S_, LRU_WIDTH) + ba)
    gate_x = jax.nn.sigmoid(jnp.einsum('bshi,hij->bshj', xh, wx).reshape(B_, S_, LRU_WIDTH) + bx)
    log_a = -LRU_C * gate_a.astype(jnp.float32) * jax.nn.softplus(-lam.astype(jnp.float32))
    a = jnp.exp(log_a)
    mult = jnp.sqrt(-jnp.expm1(2.0 * log_a))
    b_in = mult * (gate_x * x).astype(jnp.float32)

    def combine(left, right):
        a_l, b_l = left
        a_r, b_r = right
        return a_l * a_r, a_r * b_l + b_r

    _, h = lax.associative_scan(combine, (a, b_in), axis=1)
    return h.astype(x.dtype)


def rope_cos_sin(positions):
    half = MLA_ROPE // 2
    inv_freq = ROPE_THETA ** (-jnp.arange(half, dtype=jnp.float32) / half)
    ang = positions.astype(jnp.float32)[..., None] * inv_freq
    return jnp.cos(ang), jnp.sin(ang)


def apply_rope(x, cos, sin):
    x1, x2 = jnp.split(x.astype(jnp.float32), 2, axis=-1)
    return jnp.concatenate([x1 * cos - x2 * sin, x2 * cos + x1 * sin], axis=-1).astype(x.dtype)


def mla(q_lat, kv_lat, k_rope, positions, q_norm_g, w_uq, kv_norm_g, w_ukv):
    B_, S_, _ = q_lat.shape
    d_qk = MLA_NOPE + MLA_ROPE
    q = (rms_norm(q_lat, q_norm_g) @ w_uq).reshape(B_, S_, MLA_HEADS, d_qk)
    kv = (rms_norm(kv_lat, kv_norm_g) @ w_ukv).reshape(B_, S_, MLA_HEADS, MLA_NOPE + MLA_V)
    q_nope, q_pe = q[..., :MLA_NOPE], q[..., MLA_NOPE:]
    k_nope, v = kv[..., :MLA_NOPE], kv[..., MLA_NOPE:]
    cos, sin = rope_cos_sin(positions)
    q_pe = apply_rope(q_pe, cos[:, :, None, :], sin[:, :, None, :])
    k_pe = apply_rope(k_rope, cos, sin)
    q = jnp.concatenate([q_nope, q_pe], axis=-1)
    k = jnp.concatenate([k_nope, jnp.broadcast_to(k_pe[:, :, None, :], (B_, S_, MLA_HEADS, MLA_ROPE))], axis=-1)
    scale = d_qk ** -0.5
    n_blk = S_ // Q_BLOCK
    q_blocks = q.reshape(B_, n_blk, Q_BLOCK, MLA_HEADS, d_qk).transpose(1, 0, 2, 3, 4)
    k_chunk = jnp.arange(S_) // CHUNK

    def attend(args):
        qb, blk = args
        q_chunk = (blk * Q_BLOCK + jnp.arange(Q_BLOCK)) // CHUNK
        s = jnp.einsum('bqhd,bkhd->bhqk', qb, k).astype(jnp.float32) * scale
        mask = k_chunk[None, :] <= q_chunk[:, None]
        s = jnp.where(mask[None, None], s, -jnp.inf)
        p = jax.nn.softmax(s, axis=-1).astype(v.dtype)
        return jnp.einsum('bhqk,bkhd->bqhd', p, v)

    o = lax.map(attend, (q_blocks, jnp.arange(n_blk)))
    return o.transpose(1, 0, 2, 3, 4).reshape(B_, S_, MLA_WIDTH)


def spatial_gating(u, v, norm_g, norm_b, w_s, b_s):
    B_, S_, _ = u.shape
    u = jax.nn.gelu(u)
    v = layer_norm(jax.nn.gelu(v), norm_g, norm_b)
    n_blk = S_ // SGU_BLOCK
    vb = v.reshape(B_, n_blk, SGU_BLOCK, SGU_GROUPS, SGU_GROUP_DIM)
    pos_chunk = jnp.arange(SGU_BLOCK) // CHUNK
    mask = pos_chunk[:, None] >= pos_chunk[None, :]
    w = jnp.where(mask[None], w_s, jnp.zeros_like(w_s))
    mixed = jnp.einsum('gij,bnjgc->bnigc', w, vb) + b_s.T[None, None, :, :, None]
    return u * mixed.reshape(B_, S_, SGU_WIDTH)


def hybrid_layer(x, positions, pre_g, w_in, conv_w, conv_b, wa, ba, wx, bx, lam,
                 q_norm_g, w_uq, kv_norm_g, w_ukv, sgu_g, sgu_bn, sgu_w, sgu_b,
                 branch_g, w_out, post_g):
    h = rms_norm(x, pre_g)
    proj = h @ w_in
    offsets = [int(o) for o in np.cumsum(IN_SPLITS)[:-1]]
    xa, ga, q_lat, kv_lat, k_rope, gb, u, v, gc = jnp.split(proj, offsets, axis=-1)
    ya = rg_lru(causal_depthwise_conv(xa, conv_w, conv_b), wa, ba, wx, bx, lam) * jax.nn.silu(ga)
    yb = mla(q_lat, kv_lat, k_rope, positions, q_norm_g, w_uq, kv_norm_g, w_ukv) * jax.nn.silu(gb)
    yc = spatial_gating(u, v, sgu_g, sgu_bn, sgu_w, sgu_b) * jax.nn.silu(gc)
    y = jnp.concatenate([
        rms_norm(ya, branch_g[:LRU_WIDTH]),
        rms_norm(yb, branch_g[LRU_WIDTH:LRU_WIDTH + MLA_WIDTH]),
        rms_norm(yc, branch_g[LRU_WIDTH + MLA_WIDTH:]),
    ], axis=-1)
    return x + rms_norm(y @ w_out, post_g)


def setup_inputs(seed: int = 0) -> dict:
    key = jax.random.key(seed)
    ks = jax.random.split(key, 24)
    f32 = jnp.float32

    def nrm(k, shape, scale):
        return jax.random.normal(k, shape, f32) * scale

    def gain(k, shape):
        return 1.0 + 0.05 * jax.random.normal(k, shape, f32)

    x = jax.random.normal(ks[0], (BATCH, SEQ, D_MODEL), f32)
    offset = jax.random.randint(ks[1], (BATCH, 1), 0, 4096, dtype=jnp.int32)
    positions = (offset + jnp.arange(SEQ, dtype=jnp.int32)[None, :]).astype(jnp.int32)
    a0 = jax.random.uniform(ks[2], (DEPTH, LRU_WIDTH), f32, minval=0.9, maxval=0.999)
    s0 = a0 ** (1.0 / LRU_C)
    lru_lambda = jnp.log(s0) - jnp.log1p(-s0)
    return {
        "x": x,
        "positions": positions,
        "pre_norm_g": gain(ks[3], (DEPTH, D_MODEL)),
        "w_in": nrm(ks[4], (DEPTH, D_MODEL, D_IN), D_MODEL ** -0.5),
        "conv_w": nrm(ks[5], (DEPTH, CONV_WIDTH, LRU_WIDTH), CONV_WIDTH ** -0.5),
        "conv_b": nrm(ks[6], (DEPTH, LRU_WIDTH), 0.02),
        "lru_wa": nrm(ks[7], (DEPTH, LRU_HEADS, LRU_HEAD_DIM, LRU_HEAD_DIM), LRU_HEAD_DIM ** -0.5),
        "lru_ba": nrm(ks[8], (DEPTH, LRU_WIDTH), 0.1),
        "lru_wx": nrm(ks[9], (DEPTH, LRU_HEADS, LRU_HEAD_DIM, LRU_HEAD_DIM), LRU_HEAD_DIM ** -0.5),
        "lru_bx": nrm(ks[10], (DEPTH, LRU_WIDTH), 0.1),
        "lru_lambda": lru_lambda,
        "q_norm_g": gain(ks[11], (DEPTH, Q_RANK)),
        "w_uq": nrm(ks[12], (DEPTH, Q_RANK, MLA_HEADS * (MLA_NOPE + MLA_ROPE)), Q_RANK ** -0.5),
        "kv_norm_g": gain(ks[13], (DEPTH, KV_RANK)),
        "w_ukv": nrm(ks[14], (DEPTH, KV_RANK, MLA_HEADS * (MLA_NOPE + MLA_V)), KV_RANK ** -0.5),
        "sgu_norm_g": gain(ks[15], (DEPTH, SGU_WIDTH)),
        "sgu_norm_b": nrm(ks[16], (DEPTH, SGU_WIDTH), 0.02),
        "sgu_w": nrm(ks[17], (DEPTH, SGU_GROUPS, SGU_BLOCK, SGU_BLOCK), SGU_BLOCK ** -0.5),
        "sgu_b": gain(ks[18], (DEPTH, SGU_GROUPS, SGU_BLOCK)),
        "branch_norm_g": gain(ks[19], (DEPTH, D_MIX)),
        "w_out": nrm(ks[20], (DEPTH, D_MIX, D_MODEL), D_MIX ** -0.5),
        "post_norm_g": gain(ks[21], (DEPTH, D_MODEL)),
    }


def reference(x, positions, pre_norm_g, w_in, conv_w, conv_b, lru_wa, lru_ba, lru_wx, lru_bx,
              lru_lambda, q_norm_g, w_uq, kv_norm_g, w_ukv, sgu_norm_g, sgu_norm_b, sgu_w, sgu_b,
              branch_norm_g, w_out, post_norm_g):
    h = x
    for l in range(DEPTH):
        h = hybrid_layer(h, positions, pre_norm_g[l], w_in[l], conv_w[l], conv_b[l],
                         lru_wa[l], lru_ba[l], lru_wx[l], lru_bx[l], lru_lambda[l],
                         q_norm_g[l], w_uq[l], kv_norm_g[l], w_ukv[l],
                         sgu_norm_g[l], sgu_norm_b[l], sgu_w[l], sgu_b[l],
                         branch_norm_g[l], w_out[l], post_norm_g[l])
    return h
```

```python
import functools
import math

import numpy as np
import jax
import jax.numpy as jnp
from jax import lax
from jax.experimental import pallas as pl
from jax.experimental.pallas import tpu as pltpu

F32 = jnp.float32
BF16 = jnp.bfloat16

EPS = 1e-6
CHUNK = 64

LRU_WIDTH = 384
LRU_HEADS = 6
LRU_HEAD_DIM = 64
CONV_WIDTH = 4
LRU_C = 8.0

MLA_HEADS = 6
MLA_NOPE = 64
MLA_ROPE = 32
MLA_V = 64
MLA_QK = MLA_NOPE + MLA_ROPE
MLA_WIDTH = MLA_HEADS * MLA_V
Q_RANK = 192
KV_RANK = 128
ROPE_THETA = 10000.0

SGU_WIDTH = 256
SGU_GROUPS = 4
SGU_GROUP_DIM = 64
SGU_BLOCK = 128

LANES = 128
QK_PAD = 128
Q_LAT_PAD = 256

COL_A = 0
COL_LAT = COL_A + 2 * LRU_WIDTH
COL_KR = COL_LAT + Q_LAT_PAD + KV_RANK
COL_C = COL_KR + 2 * LANES
N_TOK = COL_C + 3 * SGU_WIDTH

ROW_TILE = 512
ATT_TQ = 256
ATT_TK = 256
VMEM_LIMIT = 48 * 1024 * 1024

NT_DIMS = (((1,), (1,)), ((), ()))
TN_DIMS = (((0,), (0,)), ((), ()))


def _cparams(*sem):
    return pltpu.CompilerParams(dimension_semantics=sem, vmem_limit_bytes=VMEM_LIMIT)


def _silu(x):
    return x * jax.nn.sigmoid(x)


def _rope_table_kernel(pos_ref, inv_ref, ct_ref, st_ref, c128_ref, s128_ref):
    pos = pos_ref[0].astype(F32)
    ang = inv_ref[...] * pos
    cos = jnp.cos(ang)
    sin = jnp.sin(ang)
    row = lax.broadcasted_iota(jnp.int32, ang.shape, 0)
    sin_signed = jnp.where(row < MLA_ROPE // 2, -sin, sin)
    ct_ref[0] = cos
    st_ref[0] = sin_signed
    ts = ang.shape[1]
    zeros_lo = jnp.zeros((MLA_NOPE, ts), F32)
    zeros_hi = jnp.zeros((LANES - MLA_QK, ts), F32)
    c128_ref[0] = jnp.concatenate([zeros_lo, cos, zeros_hi], axis=0).T
    s128_ref[0] = jnp.concatenate([zeros_lo, sin_signed, zeros_hi], axis=0).T


def _rope_tables(positions, ts):
    B, S = positions.shape
    half = MLA_ROPE // 2
    inv_freq = ROPE_THETA ** (-jnp.arange(half, dtype=F32) / half)
    inv2 = jnp.concatenate([inv_freq, inv_freq]).reshape(MLA_ROPE, 1)
    pos3 = positions.reshape(B, 1, S)
    return pl.pallas_call(
        _rope_table_kernel,
        grid=(B, S // ts),
        in_specs=[
            pl.BlockSpec((1, 1, ts), lambda b, i: (b, 0, i)),
            pl.BlockSpec((MLA_ROPE, 1), lambda b, i: (0, 0)),
        ],
        out_specs=[
            pl.BlockSpec((1, MLA_ROPE, ts), lambda b, i: (b, 0, i)),
            pl.BlockSpec((1, MLA_ROPE, ts), lambda b, i: (b, 0, i)),
            pl.BlockSpec((1, ts, LANES), lambda b, i: (b, i, 0)),
            pl.BlockSpec((1, ts, LANES), lambda b, i: (b, i, 0)),
        ],
        out_shape=[
            jax.ShapeDtypeStruct((B, MLA_ROPE, S), F32),
            jax.ShapeDtypeStruct((B, MLA_ROPE, S), F32),
            jax.ShapeDtypeStruct((B, S, LANES), F32),
            jax.ShapeDtypeStruct((B, S, LANES), F32),
        ],
        compiler_params=_cparams("parallel", "parallel"),
        name="rope_tables",
    )(pos3, inv2)


def _inproj_kernel(x_ref, g_ref, wtok_ref, wgbt_ref, a_ref, lat_ref, kr_ref, c_ref, gbt_ref):
    x = x_ref[0]
    ms = jnp.mean(x * x, axis=-1, keepdims=True)
    h = (x * lax.rsqrt(ms + EPS) * g_ref[...]).astype(BF16)

    def proj(lo, hi):
        return jnp.dot(h, wtok_ref[:, lo:hi], preferred_element_type=F32)

    a_ref[0] = proj(COL_A, COL_LAT)
    lat_ref[0] = proj(COL_LAT, COL_KR)
    kr_ref[0] = proj(COL_KR, COL_C)
    c_ref[0] = proj(COL_C, N_TOK)
    gbt_ref[0] = lax.dot_general(wgbt_ref[...], h, NT_DIMS, preferred_element_type=F32)


def _inproj(x, pre_g, w_tok, w_gbt, tm):
    B, S, D = x.shape
    n_lat = COL_KR - COL_LAT
    return pl.pallas_call(
        _inproj_kernel,
        grid=(B, S // tm),
        in_specs=[
            pl.BlockSpec((1, tm, D), lambda b, i: (b, i, 0)),
            pl.BlockSpec((1, D), lambda b, i: (0, 0)),
            pl.BlockSpec((D, N_TOK), lambda b, i: (0, 0)),
            pl.BlockSpec((MLA_WIDTH, D), lambda b, i: (0, 0)),
        ],
        out_specs=[
            pl.BlockSpec((1, tm, 2 * LRU_WIDTH), lambda b, i: (b, i, 0)),
            pl.BlockSpec((1, tm, n_lat), lambda b, i: (b, i, 0)),
            pl.BlockSpec((1, tm, 2 * LANES), lambda b, i: (b, i, 0)),
            pl.BlockSpec((1, tm, 3 * SGU_WIDTH), lambda b, i: (b, i, 0)),
            pl.BlockSpec((1, MLA_WIDTH, tm), lambda b, i: (b, 0, i)),
        ],
        out_shape=[
            jax.ShapeDtypeStruct((B, S, 2 * LRU_WIDTH), F32),
            jax.ShapeDtypeStruct((B, S, n_lat), F32),
            jax.ShapeDtypeStruct((B, S, 2 * LANES), F32),
            jax.ShapeDtypeStruct((B, S, 3 * SGU_WIDTH), F32),
            jax.ShapeDtypeStruct((B, MLA_WIDTH, S), F32),
        ],
        compiler_params=_cparams("parallel", "parallel"),
        name="inproj",
    )(x, pre_g, w_tok, w_gbt)


def _scan8(a, b):
    row = lax.broadcasted_iota(jnp.int32, a.shape, 0)
    for d in (1, 2, 4):
        a_prev = pltpu.roll(a, d, 0)
        b_prev = pltpu.roll(b, d, 0)
        live = row >= d
        b = jnp.where(live, a * b_prev + b, b)
        a = jnp.where(live, a * a_prev, a)
    return a, b


def _rglru_kernel(a_ref, convw_ref, convb_ref, wg_ref, bg_ref, lam_ref, bng_ref, y_ref,
                  xbuf, hcar, abuf, bbuf):
    ts = a_ref.shape[1]
    halo = 8

    @pl.when(pl.program_id(1) == 0)
    def _():
        xbuf[0:halo, :] = jnp.zeros((halo, LRU_WIDTH), F32)
        hcar[...] = jnp.zeros_like(hcar)

    xa = a_ref[0, :, 0:LRU_WIDTH]
    ga = a_ref[0, :, LRU_WIDTH:2 * LRU_WIDTH]
    xbuf[halo:halo + ts, :] = xa
    xc = convb_ref[...]
    for k in range(CONV_WIDTH):
        off = halo - (CONV_WIDTH - 1) + k
        xc = xc + convw_ref[k:k + 1, :] * xbuf[off:off + ts, :]
    xbuf[0:halo, :] = xbuf[ts:ts + halo, :]

    gz = jnp.dot(xc.astype(BF16), wg_ref[...], preferred_element_type=F32) + bg_ref[...]
    gate_a = jax.nn.sigmoid(gz[:, 0:LRU_WIDTH])
    gate_x = jax.nn.sigmoid(gz[:, LRU_WIDTH:2 * LRU_WIDTH])
    nl = -lam_ref[...]
    softplus = jnp.maximum(nl, 0.0) + jnp.log(1.0 + jnp.exp(-jnp.abs(nl)))
    log_a = (-LRU_C) * gate_a * softplus
    a = jnp.exp(log_a)
    mult = jnp.sqrt(1.0 - a * a)
    abuf[...] = a
    bbuf[...] = mult * (gate_x * xc)

    def blk(i, h):
        r = pl.multiple_of(i * 8, 8)
        a8, b8 = _scan8(abuf[pl.ds(r, 8), :], bbuf[pl.ds(r, 8), :])
        rows = a8 * h + b8
        bbuf[pl.ds(r, 8), :] = rows
        return rows[7:8, :]

    hcar[...] = lax.fori_loop(0, ts // 8, blk, hcar[...])

    ya = bbuf[...] * _silu(ga)
    ms = jnp.mean(ya * ya, axis=-1, keepdims=True)
    y_ref[0] = (ya * lax.rsqrt(ms + EPS) * bng_ref[...]).astype(y_ref.dtype)


def _rglru(a_in, conv_w, conv_b, w_gate, b_gate, lam, bn_g, ts):
    B, S, _ = a_in.shape
    full = lambda shape: pl.BlockSpec(shape, lambda b, i: (0,) * len(shape))
    return pl.pallas_call(
        _rglru_kernel,
        grid=(B, S // ts),
        in_specs=[
            pl.BlockSpec((1, ts, 2 * LRU_WIDTH), lambda b, i: (b, i, 0)),
            full((CONV_WIDTH, LRU_WIDTH)),
            full((1, LRU_WIDTH)),
            full((LRU_WIDTH, 2 * LRU_WIDTH)),
            full((1, 2 * LRU_WIDTH)),
            full((1, LRU_WIDTH)),
            full((1, LRU_WIDTH)),
        ],
        out_specs=pl.BlockSpec((1, ts, LRU_WIDTH), lambda b, i: (b, i, 0)),
        out_shape=jax.ShapeDtypeStruct((B, S, LRU_WIDTH), BF16),
        scratch_shapes=[
            pltpu.VMEM((ts + 8, LRU_WIDTH), F32),
            pltpu.VMEM((1, LRU_WIDTH), F32),
            pltpu.VMEM((ts, LRU_WIDTH), F32),
            pltpu.VMEM((ts, LRU_WIDTH), F32),
        ],
        compiler_params=_cparams("parallel", "arbitrary"),
        name="rglru",
    )(a_in, conv_w, conv_b, w_gate, b_gate, lam, bn_g)


def _mla_prep_kernel(lat_ref, kr_ref, ct_ref, st_ref, c128_ref, s128_ref, qg_ref, kvg_ref,
                     wqt_ref, wk_ref, wvt_ref, qt_ref, k_ref, vt_ref):
    ts = lat_ref.shape[1]
    q_lat = lat_ref[0, :, 0:Q_LAT_PAD]
    kv_lat = lat_ref[0, :, Q_LAT_PAD:Q_LAT_PAD + KV_RANK]
    q_ms = jnp.sum(q_lat * q_lat, axis=-1, keepdims=True) * (1.0 / Q_RANK)
    qn = (q_lat * lax.rsqrt(q_ms + EPS) * qg_ref[...]).astype(BF16)
    kv_ms = jnp.mean(kv_lat * kv_lat, axis=-1, keepdims=True)
    kvn = (kv_lat * lax.rsqrt(kv_ms + EPS) * kvg_ref[...]).astype(BF16)

    qt = lax.dot_general(wqt_ref[...], qn, NT_DIMS, preferred_element_type=F32)
    ct = ct_ref[0]
    st = st_ref[0]
    half = MLA_ROPE // 2
    q_scale = (MLA_QK ** -0.5) * math.log2(math.e)
    pad = jnp.zeros((QK_PAD - MLA_QK, ts), F32)
    for h in range(MLA_HEADS):
        base = h * MLA_QK
        nope = qt[base:base + MLA_NOPE, :]
        pe = qt[base + MLA_NOPE:base + MLA_QK, :]
        pe_sw = jnp.concatenate([pe[half:, :], pe[:half, :]], axis=0)
        pe = pe * ct + pe_sw * st
        qh = jnp.concatenate([nope, pe, pad], axis=0) * q_scale
        qt_ref[0, h] = qh.astype(qt_ref.dtype)

    kn = jnp.dot(kvn, wk_ref[...], preferred_element_type=F32)
    pe128 = kr_ref[0, :, 0:LANES] * c128_ref[0] + kr_ref[0, :, LANES:2 * LANES] * s128_ref[0]
    for h in range(MLA_HEADS):
        k_ref[0, h] = (kn[:, h * LANES:(h + 1) * LANES] + pe128).astype(k_ref.dtype)

    vt = lax.dot_general(wvt_ref[...], kvn, NT_DIMS, preferred_element_type=F32)
    for h in range(MLA_HEADS):
        vt_ref[0, h] = vt[h * MLA_V:(h + 1) * MLA_V, :].astype(vt_ref.dtype)


def _mla_prep(lat, kr, ct, st, c128, s128, q_g, kv_g, wqt, wk, wvt, ts):
    B, S, n_lat = lat.shape
    full = lambda shape: pl.BlockSpec(shape, lambda b, i: (0,) * len(shape))
    return pl.pallas_call(
        _mla_prep_kernel,
        grid=(B, S // ts),
        in_specs=[
            pl.BlockSpec((1, ts, n_lat), lambda b, i: (b, i, 0)),
            pl.BlockSpec((1, ts, 2 * LANES), lambda b, i: (b, i, 0)),
            pl.BlockSpec((1, MLA_ROPE, ts), lambda b, i: (b, 0, i)),
            pl.BlockSpec((1, MLA_ROPE, ts), lambda b, i: (b, 0, i)),
            pl.BlockSpec((1, ts, LANES), lambda b, i: (b, i, 0)),
            pl.BlockSpec((1, ts, LANES), lambda b, i: (b, i, 0)),
            full((1, Q_LAT_PAD)),
            full((1, KV_RANK)),
            full((MLA_HEADS * MLA_QK, Q_LAT_PAD)),
            full((KV_RANK, MLA_HEADS * LANES)),
            full((MLA_HEADS * MLA_V, KV_RANK)),
        ],
        out_specs=[
            pl.BlockSpec((1, MLA_HEADS, QK_PAD, ts), lambda b, i: (b, 0, 0, i)),
            pl.BlockSpec((1, MLA_HEADS, ts, QK_PAD), lambda b, i: (b, 0, i, 0)),
            pl.BlockSpec((1, MLA_HEADS, MLA_V, ts), lambda b, i: (b, 0, 0, i)),
        ],
        out_shape=[
            jax.ShapeDtypeStruct((B, MLA_HEADS, QK_PAD, S), BF16),
            jax.ShapeDtypeStruct((B, MLA_HEADS, S, QK_PAD), BF16),
            jax.ShapeDtypeStruct((B, MLA_HEADS, MLA_V, S), BF16),
        ],
        compiler_params=_cparams("parallel", "parallel"),
        name="mla_prep",
    )(lat, kr, ct, st, c128, s128, q_g, kv_g, wqt, wk, wvt)


def _attn_kernel(qt_ref, k_ref, vt_ref, o_ref):
    qi = pl.program_id(2)
    tq = qt_ref.shape[3]
    tk = tq
    qt = qt_ref[0, 0]

    def scores(j):
        ks = pl.multiple_of(j * tk, tk)
        k = k_ref[0, 0, pl.ds(ks, tk), :]
        return jnp.dot(k, qt, preferred_element_type=F32), ks

    def update(s, ks, carry):
        m, l, acc = carry
        m_new = jnp.maximum(m, jnp.max(s, axis=0, keepdims=True))
        p = jnp.exp2(s - m_new)
        alpha = jnp.exp2(m - m_new)
        l = alpha * l + jnp.sum(p, axis=0, keepdims=True)
        v = vt_ref[0, 0, :, pl.ds(ks, tk)]
        acc = alpha * acc + jnp.dot(v, p.astype(BF16), preferred_element_type=F32)
        return m_new, l, acc

    def body(j, carry):
        s, ks = scores(j)
        return update(s, ks, carry)

    init = (jnp.full((1, tq), -jnp.inf, F32), jnp.zeros((1, tq), F32),
            jnp.zeros((MLA_V, tq), F32))
    carry = lax.fori_loop(0, qi, body, init)

    s, ks = scores(qi)
    kc = lax.broadcasted_iota(jnp.int32, s.shape, 0) // CHUNK
    qc = lax.broadcasted_iota(jnp.int32, s.shape, 1) // CHUNK
    s = jnp.where(kc <= qc, s, -jnp.inf)
    m, l, acc = update(s, ks, carry)
    o_ref[0, 0] = (acc / l).astype(o_ref.dtype)


def _attention(qt, k, vt, tq):
    B, H, _, S = qt.shape
    return pl.pallas_call(
        _attn_kernel,
        grid=(B, H, S // tq),
        in_specs=[
            pl.BlockSpec((1, 1, QK_PAD, tq), lambda b, h, i: (b, h, 0, i)),
            pl.BlockSpec((1, 1, S, QK_PAD), lambda b, h, i: (b, h, 0, 0)),
            pl.BlockSpec((1, 1, MLA_V, S), lambda b, h, i: (b, h, 0, 0)),
        ],
        out_specs=pl.BlockSpec((1, 1, MLA_V, tq), lambda b, h, i: (b, h, 0, i)),
        out_shape=jax.ShapeDtypeStruct((B, H, MLA_V, S), F32),
        compiler_params=_cparams("parallel", "parallel", "arbitrary"),
        name="attention",
    )(qt, k, vt)


def _gelu(x):
    c = math.sqrt(2.0 / math.pi)
    return 0.5 * x * (1.0 + jnp.tanh(c * (x + 0.044715 * (x * x * x))))


def _sgu_kernel(c_ref, ng_ref, nb_ref, w_ref, bias_ref, bng_ref, y_ref):
    tm = c_ref.shape[1]
    u = _gelu(c_ref[0, :, 0:SGU_WIDTH])
    v = _gelu(c_ref[0, :, SGU_WIDTH:2 * SGU_WIDTH])
    gc = c_ref[0, :, 2 * SGU_WIDTH:3 * SGU_WIDTH]
    mu = jnp.mean(v, axis=-1, keepdims=True)
    vc = v - mu
    var = jnp.mean(vc * vc, axis=-1, keepdims=True)
    vn = (vc * lax.rsqrt(var + EPS) * ng_ref[...] + nb_ref[...]).astype(BF16)

    w = w_ref[...]
    wi = (lax.broadcasted_iota(jnp.int32, w.shape, 0) % SGU_BLOCK) // CHUNK
    wj = lax.broadcasted_iota(jnp.int32, w.shape, 1) // CHUNK
    w = jnp.where(wi >= wj, w, 0.0).astype(BF16)

    lane_group = lax.broadcasted_iota(jnp.int32, (SGU_BLOCK, SGU_WIDTH), 1) // SGU_GROUP_DIM
    mixed = []
    for n in range(tm // SGU_BLOCK):
        r = jnp.dot(w, vn[n * SGU_BLOCK:(n + 1) * SGU_BLOCK, :], preferred_element_type=F32)
        m = r[0:SGU_BLOCK, :]
        for g in range(1, SGU_GROUPS):
            m = jnp.where(lane_group == g, r[g * SGU_BLOCK:(g + 1) * SGU_BLOCK, :], m)
        mixed.append(m + bias_ref[...])
    mixed = jnp.concatenate(mixed, axis=0)

    yc = u * mixed * _silu(gc)
    ms = jnp.mean(yc * yc, axis=-1, keepdims=True)
    y_ref[0] = (yc * lax.rsqrt(ms + EPS) * bng_ref[...]).astype(y_ref.dtype)


def _sgu(c_in, n_g, n_b, w_stack, bias, bn_g, tm):
    B, S, _ = c_in.shape
    full = lambda shape: pl.BlockSpec(shape, lambda b, i: (0,) * len(shape))
    return pl.pallas_call(
        _sgu_kernel,
        grid=(B, S // tm),
        in_specs=[
            pl.BlockSpec((1, tm, 3 * SGU_WIDTH), lambda b, i: (b, i, 0)),
            full((1, SGU_WIDTH)),
            full((1, SGU_WIDTH)),
            full((SGU_GROUPS * SGU_BLOCK, SGU_BLOCK)),
            full((SGU_BLOCK, SGU_WIDTH)),
            full((1, SGU_WIDTH)),
        ],
        out_specs=pl.BlockSpec((1, tm, SGU_WIDTH), lambda b, i: (b, i, 0)),
        out_shape=jax.ShapeDtypeStruct((B, S, SGU_WIDTH), BF16),
        compiler_params=_cparams("parallel", "parallel"),
        name="sgu",
    )(c_in, n_g, n_b, w_stack, bias, bn_g)


def _outproj_kernel(x_ref, ya_ref, ot_ref, gbt_ref, yc_ref, bngb_ref, w1_ref, w2_ref, w3_ref,
                    pg_ref, out_ref):
    ybt = ot_ref[0] * _silu(gbt_ref[0])
    ms = jnp.mean(ybt * ybt, axis=0, keepdims=True)
    ybt = (ybt * lax.rsqrt(ms + EPS) * bngb_ref[...]).astype(BF16)
    y = jnp.dot(ya_ref[0], w1_ref[...], preferred_element_type=F32)
    y = y + lax.dot_general(ybt, w2_ref[...], TN_DIMS, preferred_element_type=F32)
    y = y + jnp.dot(yc_ref[0], w3_ref[...], preferred_element_type=F32)
    ms = jnp.mean(y * y, axis=-1, keepdims=True)
    out_ref[0] = x_ref[0] + y * lax.rsqrt(ms + EPS) * pg_ref[...]


def _outproj(x, ya, ot, gbt, yc, bng_b, w1, w2, w3, post_g, tm):
    B, S, D = x.shape
    full = lambda shape: pl.BlockSpec(shape, lambda b, i: (0,) * len(shape))
    return pl.pallas_call(
        _outproj_kernel,
        grid=(B, S // tm),
        in_specs=[
            pl.BlockSpec((1, tm, D), lambda b, i: (b, i, 0)),
            pl.BlockSpec((1, tm, LRU_WIDTH), lambda b, i: (b, i, 0)),
            pl.BlockSpec((1, MLA_WIDTH, tm), lambda b, i: (b, 0, i)),
            pl.BlockSpec((1, MLA_WIDTH, tm), lambda b, i: (b, 0, i)),
            pl.BlockSpec((1, tm, SGU_WIDTH), lambda b, i: (b, i, 0)),
            full((MLA_WIDTH, 1)),
            full((LRU_WIDTH, D)),
            full((MLA_WIDTH, D)),
            full((SGU_WIDTH, D)),
            full((1, D)),
        ],
        out_specs=pl.BlockSpec((1, tm, D), lambda b, i: (b, i, 0)),
        out_shape=jax.ShapeDtypeStruct((B, S, D), F32),
        compiler_params=_cparams("parallel", "parallel"),
        name="outproj",
    )(x, ya, ot, gbt, yc, bng_b, w1, w2, w3, post_g)


def _layer_params(l, pre_norm_g, w_in, conv_w, conv_b, lru_wa, lru_ba, lru_wx, lru_bx, lru_lambda,
                  q_norm_g, w_uq, kv_norm_g, w_ukv, sgu_norm_g, sgu_norm_b, sgu_w, sgu_b,
                  branch_norm_g, w_out, post_norm_g):
    D = w_in.shape[1]
    offs = np.cumsum([0, LRU_WIDTH, LRU_WIDTH, Q_RANK, KV_RANK, MLA_ROPE, MLA_WIDTH,
                      SGU_WIDTH, SGU_WIDTH, SGU_WIDTH])
    win = w_in[l]
    cols = lambda i: win[:, offs[i]:offs[i + 1]]
    zeros = lambda n: jnp.zeros((D, n), win.dtype)
    w_kr = cols(4)
    half = MLA_ROPE // 2
    w_kr_sw = jnp.concatenate([w_kr[:, half:], w_kr[:, :half]], axis=1)
    place = lambda w: jnp.concatenate([zeros(MLA_NOPE), w, zeros(LANES - MLA_QK)], axis=1)
    w_tok = jnp.concatenate([
        cols(0), cols(1),
        cols(2), zeros(Q_LAT_PAD - Q_RANK), cols(3),
        place(w_kr), place(w_kr_sw),
        cols(6), cols(7), cols(8),
    ], axis=1).astype(BF16)
    w_gbt = cols(5).T.astype(BF16)

    def block_diag(w):
        eye = jnp.eye(LRU_HEADS, dtype=w.dtype)
        return jnp.einsum('hij,hg->higj', w, eye).reshape(LRU_WIDTH, LRU_WIDTH)

    w_gate = jnp.concatenate([block_diag(lru_wa[l]), block_diag(lru_wx[l])], axis=1).astype(BF16)
    b_gate = jnp.concatenate([lru_ba[l], lru_bx[l]]).reshape(1, -1)

    q_g = jnp.concatenate([q_norm_g[l], jnp.zeros((Q_LAT_PAD - Q_RANK,), F32)]).reshape(1, -1)
    wqt = jnp.concatenate([w_uq[l], jnp.zeros((Q_LAT_PAD - Q_RANK, w_uq.shape[2]), F32)], axis=0)
    wqt = wqt.T.astype(BF16)
    wkv = w_ukv[l].reshape(KV_RANK, MLA_HEADS, MLA_NOPE + MLA_V)
    wk = jnp.concatenate([wkv[:, :, :MLA_NOPE],
                          jnp.zeros((KV_RANK, MLA_HEADS, LANES - MLA_NOPE), F32)], axis=2)
    wk = wk.reshape(KV_RANK, MLA_HEADS * LANES).astype(BF16)
    wvt = wkv[:, :, MLA_NOPE:].reshape(KV_RANK, MLA_HEADS * MLA_V).T.astype(BF16)

    w_stack = sgu_w[l].reshape(SGU_GROUPS * SGU_BLOCK, SGU_BLOCK)
    sgu_bias = jnp.repeat(sgu_b[l].T, SGU_GROUP_DIM, axis=1)

    bng = branch_norm_g[l]
    wo = w_out[l].astype(BF16)
    return dict(
        pre_g=pre_norm_g[l].reshape(1, -1), w_tok=w_tok, w_gbt=w_gbt,
        conv_w=conv_w[l], conv_b=conv_b[l].reshape(1, -1), w_gate=w_gate, b_gate=b_gate,
        lam=lru_lambda[l].reshape(1, -1),
        q_g=q_g, kv_g=kv_norm_g[l].reshape(1, -1), wqt=wqt, wk=wk, wvt=wvt,
        sgu_g=sgu_norm_g[l].reshape(1, -1), sgu_nb=sgu_norm_b[l].reshape(1, -1),
        w_stack=w_stack, sgu_bias=sgu_bias,
        bng_a=bng[:LRU_WIDTH].reshape(1, -1),
        bng_b=bng[LRU_WIDTH:LRU_WIDTH + MLA_WIDTH].reshape(-1, 1),
        bng_c=bng[LRU_WIDTH + MLA_WIDTH:].reshape(1, -1),
        w1=wo[:LRU_WIDTH], w2=wo[LRU_WIDTH:LRU_WIDTH + MLA_WIDTH], w3=wo[LRU_WIDTH + MLA_WIDTH:],
        post_g=post_norm_g[l].reshape(1, -1),
    )


def kernel(x, positions, pre_norm_g, w_in, conv_w, conv_b, lru_wa, lru_ba, lru_wx, lru_bx, lru_lambda,
           q_norm_g, w_uq, kv_norm_g, w_ukv, sgu_norm_g, sgu_norm_b, sgu_w, sgu_b, branch_norm_g,
           w_out, post_norm_g):
    B, S, D = x.shape
    depth = w_in.shape[0]
    tm = min(ROW_TILE, S)
    tq = min(ATT_TQ, S)
    assert S % tm == 0 and S % tq == 0 and tq % CHUNK == 0 and tm % SGU_BLOCK == 0

    ct, st, c128, s128 = _rope_tables(positions, tm)
    h = x
    for l in range(depth):
        p = _layer_params(l, pre_norm_g, w_in, conv_w, conv_b, lru_wa, lru_ba, lru_wx, lru_bx,
                          lru_lambda, q_norm_g, w_uq, kv_norm_g, w_ukv, sgu_norm_g, sgu_norm_b,
                          sgu_w, sgu_b, branch_norm_g, w_out, post_norm_g)
        a_in, lat, kr, c_in, gbt = _inproj(h, p['pre_g'], p['w_tok'], p['w_gbt'], tm)
        ya = _rglru(a_in, p['conv_w'], p['conv_b'], p['w_gate'], p['b_gate'], p['lam'], p['bng_a'], tm)
        qt, k, vt = _mla_prep(lat, kr, ct, st, c128, s128, p['q_g'], p['kv_g'],
                              p['wqt'], p['wk'], p['wvt'], tm)
        ot = _attention(qt, k, vt, tq).reshape(B, MLA_WIDTH, S)
        yc = _sgu(c_in, p['sgu_g'], p['sgu_nb'], p['w_stack'], p['sgu_bias'], p['bng_c'], tm)
        h = _outproj(h, ya, ot, gbt, yc, p['bng_b'], p['w1'], p['w2'], p['w3'], p['post_g'], tm)
    return h
```

```python
import functools
import math

import numpy as np
import jax
import jax.numpy as jnp
from jax import lax
from jax.experimental import pallas as pl
from jax.experimental.pallas import tpu as pltpu

F32 = jnp.float32
BF16 = jnp.bfloat16

EPS = 1e-6
CHUNK = 64

LRU_WIDTH = 384
LRU_HEADS = 6
LRU_HEAD_DIM = 64
CONV_WIDTH = 4
LRU_C = 8.0

MLA_HEADS = 6
MLA_NOPE = 64
MLA_ROPE = 32
MLA_V = 64
MLA_QK = MLA_NOPE + MLA_ROPE
MLA_WIDTH = MLA_HEADS * MLA_V
Q_RANK = 192
KV_RANK = 128
ROPE_THETA = 10000.0

SGU_WIDTH = 256
SGU_GROUPS = 4
SGU_GROUP_DIM = 64
SGU_BLOCK = 128

LANES = 128
QK_PAD = 128
Q_LAT_PAD = 256

COL_A = 0
COL_LAT = COL_A + 2 * LRU_WIDTH
COL_KR = COL_LAT + Q_LAT_PAD + KV_RANK
COL_C = COL_KR + 2 * LANES
N_TOK = COL_C + 3 * SGU_WIDTH

ROW_TILE = 512
ATT_TQ = 512
ATT_KBIG = 4
VMEM_LIMIT = 48 * 1024 * 1024

NT_DIMS = (((1,), (1,)), ((), ()))
TN_DIMS = (((0,), (0,)), ((), ()))


def _cparams(*sem):
    return pltpu.CompilerParams(dimension_semantics=sem, vmem_limit_bytes=VMEM_LIMIT)


def _silu(x):
    return x * jax.nn.sigmoid(x)


def _rope_table_kernel(pos_ref, inv_ref, ct_ref, st_ref, c128_ref, s128_ref):
    pos = pos_ref[0].astype(F32)
    ang = inv_ref[...] * pos
    cos = jnp.cos(ang)
    sin = jnp.sin(ang)
    row = lax.broadcasted_iota(jnp.int32, ang.shape, 0)
    sin_signed = jnp.where(row < MLA_ROPE // 2, -sin, sin)
    ct_ref[0] = cos
    st_ref[0] = sin_signed
    ts = ang.shape[1]
    zeros_lo = jnp.zeros((MLA_NOPE, ts), F32)
    zeros_hi = jnp.zeros((LANES - MLA_QK, ts), F32)
    c128_ref[0] = jnp.concatenate([zeros_lo, cos, zeros_hi], axis=0).T
    s128_ref[0] = jnp.concatenate([zeros_lo, sin_signed, zeros_hi], axis=0).T


def _rope_tables(positions, ts):
    B, S = positions.shape
    half = MLA_ROPE // 2
    inv_freq = ROPE_THETA ** (-jnp.arange(half, dtype=F32) / half)
    inv2 = jnp.concatenate([inv_freq, inv_freq]).reshape(MLA_ROPE, 1)
    pos3 = positions.reshape(B, 1, S)
    return pl.pallas_call(
        _rope_table_kernel,
        grid=(B, S // ts),
        in_specs=[
            pl.BlockSpec((1, 1, ts), lambda b, i: (b, 0, i)),
            pl.BlockSpec((MLA_ROPE, 1), lambda b, i: (0, 0)),
        ],
        out_specs=[
            pl.BlockSpec((1, MLA_ROPE, ts), lambda b, i: (b, 0, i)),
            pl.BlockSpec((1, MLA_ROPE, ts), lambda b, i: (b, 0, i)),
            pl.BlockSpec((1, ts, LANES), lambda b, i: (b, i, 0)),
            pl.BlockSpec((1, ts, LANES), lambda b, i: (b, i, 0)),
        ],
        out_shape=[
            jax.ShapeDtypeStruct((B, MLA_ROPE, S), F32),
            jax.ShapeDtypeStruct((B, MLA_ROPE, S), F32),
            jax.ShapeDtypeStruct((B, S, LANES), F32),
            jax.ShapeDtypeStruct((B, S, LANES), F32),
        ],
        compiler_params=_cparams("parallel", "parallel"),
        name="rope_tables",
    )(pos3, inv2)


def _inproj_kernel(x_ref, g_ref, wtok_ref, wgbt_ref, a_ref, lat_ref, kr_ref, c_ref, gbt_ref):
    x = x_ref[0]
    ms = jnp.mean(x * x, axis=-1, keepdims=True)
    h = (x * lax.rsqrt(ms + EPS) * g_ref[...]).astype(BF16)

    def proj(lo, hi):
        return jnp.dot(h, wtok_ref[:, lo:hi], preferred_element_type=F32)

    a_ref[0] = proj(COL_A, COL_LAT)
    lat_ref[0] = proj(COL_LAT, COL_KR)
    kr_ref[0] = proj(COL_KR, COL_C)
    c_ref[0] = proj(COL_C, N_TOK)
    gbt_ref[0] = lax.dot_general(wgbt_ref[...], h, NT_DIMS, preferred_element_type=F32)


def _inproj(x, pre_g, w_tok, w_gbt, tm):
    B, S, D = x.shape
    n_lat = COL_KR - COL_LAT
    return pl.pallas_call(
        _inproj_kernel,
        grid=(B, S // tm),
        in_specs=[
            pl.BlockSpec((1, tm, D), lambda b, i: (b, i, 0)),
            pl.BlockSpec((1, D), lambda b, i: (0, 0)),
            pl.BlockSpec((D, N_TOK), lambda b, i: (0, 0)),
            pl.BlockSpec((MLA_WIDTH, D), lambda b, i: (0, 0)),
        ],
        out_specs=[
            pl.BlockSpec((1, tm, 2 * LRU_WIDTH), lambda b, i: (b, i, 0)),
            pl.BlockSpec((1, tm, n_lat), lambda b, i: (b, i, 0)),
            pl.BlockSpec((1, tm, 2 * LANES), lambda b, i: (b, i, 0)),
            pl.BlockSpec((1, tm, 3 * SGU_WIDTH), lambda b, i: (b, i, 0)),
            pl.BlockSpec((1, MLA_WIDTH, tm), lambda b, i: (b, 0, i)),
        ],
        out_shape=[
            jax.ShapeDtypeStruct((B, S, 2 * LRU_WIDTH), F32),
            jax.ShapeDtypeStruct((B, S, n_lat), F32),
            jax.ShapeDtypeStruct((B, S, 2 * LANES), F32),
            jax.ShapeDtypeStruct((B, S, 3 * SGU_WIDTH), F32),
            jax.ShapeDtypeStruct((B, MLA_WIDTH, S), F32),
        ],
        compiler_params=_cparams("parallel", "parallel"),
        name="inproj",
    )(x, pre_g, w_tok, w_gbt)


def _scan8(a, b):
    row = lax.broadcasted_iota(jnp.int32, a.shape, 0)
    for d in (1, 2, 4):
        a_prev = pltpu.roll(a, d, 0)
        b_prev = pltpu.roll(b, d, 0)
        live = row >= d
        b = jnp.where(live, a * b_prev + b, b)
        a = jnp.where(live, a * a_prev, a)
    return a, b


def _rglru_kernel(a_ref, convw_ref, convb_ref, wg_ref, bg_ref, lam_ref, bng_ref, y_ref,
                  xbuf, hcar, abuf, bbuf):
    ts = a_ref.shape[1]
    halo = 8

    @pl.when(pl.program_id(1) == 0)
    def _():
        xbuf[0:halo, :] = jnp.zeros((halo, LRU_WIDTH), F32)
        hcar[...] = jnp.zeros_like(hcar)

    xa = a_ref[0, :, 0:LRU_WIDTH]
    ga = a_ref[0, :, LRU_WIDTH:2 * LRU_WIDTH]
    xbuf[halo:halo + ts, :] = xa
    xc = convb_ref[...]
    for k in range(CONV_WIDTH):
        off = halo - (CONV_WIDTH - 1) + k
        xc = xc + convw_ref[k:k + 1, :] * xbuf[off:off + ts, :]
    xbuf[0:halo, :] = xbuf[ts:ts + halo, :]

    gz = jnp.dot(xc.astype(BF16), wg_ref[...], preferred_element_type=F32) + bg_ref[...]
    gate_a = jax.nn.sigmoid(gz[:, 0:LRU_WIDTH])
    gate_x = jax.nn.sigmoid(gz[:, LRU_WIDTH:2 * LRU_WIDTH])
    nl = -lam_ref[...]
    softplus = jnp.maximum(nl, 0.0) + jnp.log(1.0 + jnp.exp(-jnp.abs(nl)))
    log_a = (-LRU_C) * gate_a * softplus
    a = jnp.exp(log_a)
    mult = jnp.sqrt(1.0 - a * a)
    abuf[...] = a
    bbuf[...] = mult * (gate_x * xc)

    def blk(i, h):
        r = pl.multiple_of(i * 8, 8)
        a8, b8 = _scan8(abuf[pl.ds(r, 8), :], bbuf[pl.ds(r, 8), :])
        rows = a8 * h + b8
        bbuf[pl.ds(r, 8), :] = rows
        return rows[7:8, :]

    hcar[...] = lax.fori_loop(0, ts // 8, blk, hcar[...])

    ya = bbuf[...] * _silu(ga)
    ms = jnp.mean(ya * ya, axis=-1, keepdims=True)
    y_ref[0] = (ya * lax.rsqrt(ms + EPS) * bng_ref[...]).astype(y_ref.dtype)


def _rglru(a_in, conv_w, conv_b, w_gate, b_gate, lam, bn_g, ts):
    B, S, _ = a_in.shape
    full = lambda shape: pl.BlockSpec(shape, lambda b, i: (0,) * len(shape))
    return pl.pallas_call(
        _rglru_kernel,
        grid=(B, S // ts),
        in_specs=[
            pl.BlockSpec((1, ts, 2 * LRU_WIDTH), lambda b, i: (b, i, 0)),
            full((CONV_WIDTH, LRU_WIDTH)),
            full((1, LRU_WIDTH)),
            full((LRU_WIDTH, 2 * LRU_WIDTH)),
            full((1, 2 * LRU_WIDTH)),
            full((1, LRU_WIDTH)),
            full((1, LRU_WIDTH)),
        ],
        out_specs=pl.BlockSpec((1, ts, LRU_WIDTH), lambda b, i: (b, i, 0)),
        out_shape=jax.ShapeDtypeStruct((B, S, LRU_WIDTH), BF16),
        scratch_shapes=[
            pltpu.VMEM((ts + 8, LRU_WIDTH), F32),
            pltpu.VMEM((1, LRU_WIDTH), F32),
            pltpu.VMEM((ts, LRU_WIDTH), F32),
            pltpu.VMEM((ts, LRU_WIDTH), F32),
        ],
        compiler_params=_cparams("parallel", "arbitrary"),
        name="rglru",
    )(a_in, conv_w, conv_b, w_gate, b_gate, lam, bn_g)


def _mla_prep_kernel(lat_ref, kr_ref, ct_ref, st_ref, c128_ref, s128_ref, qg_ref, kvg_ref,
                     wqt_ref, wk_ref, wvt_ref, qt_ref, k_ref, vt_ref):
    ts = lat_ref.shape[1]
    q_lat = lat_ref[0, :, 0:Q_LAT_PAD]
    kv_lat = lat_ref[0, :, Q_LAT_PAD:Q_LAT_PAD + KV_RANK]
    q_ms = jnp.sum(q_lat * q_lat, axis=-1, keepdims=True) * (1.0 / Q_RANK)
    qn = (q_lat * lax.rsqrt(q_ms + EPS) * qg_ref[...]).astype(BF16)
    kv_ms = jnp.mean(kv_lat * kv_lat, axis=-1, keepdims=True)
    kvn = (kv_lat * lax.rsqrt(kv_ms + EPS) * kvg_ref[...]).astype(BF16)

    qt = lax.dot_general(wqt_ref[...], qn, NT_DIMS, preferred_element_type=F32)
    ct = ct_ref[0]
    st = st_ref[0]
    half = MLA_ROPE // 2
    q_scale = (MLA_QK ** -0.5) * math.log2(math.e)
    pad = jnp.zeros((QK_PAD - MLA_QK, ts), F32)
    for h in range(MLA_HEADS):
        base = h * MLA_QK
        nope = qt[base:base + MLA_NOPE, :]
        pe = qt[base + MLA_NOPE:base + MLA_QK, :]
        pe_sw = jnp.concatenate([pe[half:, :], pe[:half, :]], axis=0)
        pe = pe * ct + pe_sw * st
        qh = jnp.concatenate([nope, pe, pad], axis=0) * q_scale
        qt_ref[0, h] = qh.astype(qt_ref.dtype)

    kn = jnp.dot(kvn, wk_ref[...], preferred_element_type=F32)
    pe128 = kr_ref[0, :, 0:LANES] * c128_ref[0] + kr_ref[0, :, LANES:2 * LANES] * s128_ref[0]
    for h in range(MLA_HEADS):
        k_ref[0, h] = (kn[:, h * LANES:(h + 1) * LANES] + pe128).astype(k_ref.dtype)

    vt = lax.dot_general(wvt_ref[...], kvn, NT_DIMS, preferred_element_type=F32)
    for h in range(MLA_HEADS):
        vt_ref[0, h] = vt[h * MLA_V:(h + 1) * MLA_V, :].astype(vt_ref.dtype)


def _mla_prep(lat, kr, ct, st, c128, s128, q_g, kv_g, wqt, wk, wvt, ts):
    B, S, n_lat = lat.shape
    full = lambda shape: pl.BlockSpec(shape, lambda b, i: (0,) * len(shape))
    return pl.pallas_call(
        _mla_prep_kernel,
        grid=(B, S // ts),
        in_specs=[
            pl.BlockSpec((1, ts, n_lat), lambda b, i: (b, i, 0)),
            pl.BlockSpec((1, ts, 2 * LANES), lambda b, i: (b, i, 0)),
            pl.BlockSpec((1, MLA_ROPE, ts), lambda b, i: (b, 0, i)),
            pl.BlockSpec((1, MLA_ROPE, ts), lambda b, i: (b, 0, i)),
            pl.BlockSpec((1, ts, LANES), lambda b, i: (b, i, 0)),
            pl.BlockSpec((1, ts, LANES), lambda b, i: (b, i, 0)),
            full((1, Q_LAT_PAD)),
            full((1, KV_RANK)),
            full((MLA_HEADS * MLA_QK, Q_LAT_PAD)),
            full((KV_RANK, MLA_HEADS * LANES)),
            full((MLA_HEADS * MLA_V, KV_RANK)),
        ],
        out_specs=[
            pl.BlockSpec((1, MLA_HEADS, QK_PAD, ts), lambda b, i: (b, 0, 0, i)),
            pl.BlockSpec((1, MLA_HEADS, ts, QK_PAD), lambda b, i: (b, 0, i, 0)),
            pl.BlockSpec((1, MLA_HEADS, MLA_V, ts), lambda b, i: (b, 0, 0, i)),
        ],
        out_shape=[
            jax.ShapeDtypeStruct((B, MLA_HEADS, QK_PAD, S), BF16),
            jax.ShapeDtypeStruct((B, MLA_HEADS, S, QK_PAD), BF16),
            jax.ShapeDtypeStruct((B, MLA_HEADS, MLA_V, S), BF16),
        ],
        compiler_params=_cparams("parallel", "parallel"),
        name="mla_prep",
    )(lat, kr, ct, st, c128, s128, q_g, kv_g, wqt, wk, wvt)


def _attn_kernel(qt_ref, k_ref, vt_ref, o_ref, s_scr):
    qi = pl.program_id(2)
    tq = qt_ref.shape[3]
    tk = tq
    sub = 8
    qt = qt_ref[0, 0]

    def fold(x):
        return x.reshape(x.shape[0] // sub, sub, tq)

    def scores(ks):
        k = k_ref[0, 0, pl.ds(ks, tk), :]
        return jnp.dot(k, qt, preferred_element_type=F32)

    def pass1(nsub):
        def body(j, mrun):
            for i in range(nsub):
                ks = pl.multiple_of((j * nsub + i) * tk, tk)
                s = scores(ks)
                s_scr[pl.ds(ks, tk), :] = s
                mrun = jnp.maximum(mrun, jnp.max(fold(s), axis=0))
            return mrun
        return body

    n_big = qi // ATT_KBIG
    mrun = lax.fori_loop(0, n_big, pass1(ATT_KBIG), jnp.full((sub, tq), -jnp.inf, F32))
    mrun = lax.fori_loop(n_big * ATT_KBIG, qi, pass1(1), mrun)
    ks = pl.multiple_of(qi * tk, tk)
    s = scores(ks)
    kc = lax.broadcasted_iota(jnp.int32, s.shape, 0) // CHUNK
    qc = lax.broadcasted_iota(jnp.int32, s.shape, 1) // CHUNK
    s = jnp.where(kc <= qc, s, -jnp.inf)
    s_scr[pl.ds(ks, tk), :] = s
    mrun = jnp.maximum(mrun, jnp.max(fold(s), axis=0))
    m = jnp.max(mrun, axis=0, keepdims=True)

    def pass2(nsub):
        def body(j, carry):
            lrun, acc = carry
            n = nsub * tk
            ks = pl.multiple_of(j * n, n)
            p = jnp.exp2(s_scr[pl.ds(ks, n), :] - m)
            lrun = lrun + jnp.sum(fold(p), axis=0)
            v = vt_ref[0, 0, :, pl.ds(ks, n)]
            acc = acc + jnp.dot(v, p.astype(BF16), preferred_element_type=F32)
            return lrun, acc
        return body

    n_big = (qi + 1) // ATT_KBIG
    carry = (jnp.zeros((sub, tq), F32), jnp.zeros((MLA_V, tq), F32))
    carry = lax.fori_loop(0, n_big, pass2(ATT_KBIG), carry)
    lrun, acc = lax.fori_loop(n_big * ATT_KBIG, qi + 1, pass2(1), carry)
    l = jnp.sum(lrun, axis=0, keepdims=True)
    o_ref[0, 0] = (acc / l).astype(o_ref.dtype)


def _attention(qt, k, vt, tq):
    B, H, _, S = qt.shape
    return pl.pallas_call(
        _attn_kernel,
        grid=(B, H, S // tq),
        in_specs=[
            pl.BlockSpec((1, 1, QK_PAD, tq), lambda b, h, i: (b, h, 0, i)),
            pl.BlockSpec((1, 1, S, QK_PAD), lambda b, h, i: (b, h, 0, 0)),
            pl.BlockSpec((1, 1, MLA_V, S), lambda b, h, i: (b, h, 0, 0)),
        ],
        out_specs=pl.BlockSpec((1, 1, MLA_V, tq), lambda b, h, i: (b, h, 0, i)),
        out_shape=jax.ShapeDtypeStruct((B, H, MLA_V, S), F32),
        scratch_shapes=[pltpu.VMEM((S, tq), F32)],
        compiler_params=_cparams("parallel", "parallel", "arbitrary"),
        name="attention",
    )(qt, k, vt)


def _gelu(x):
    c = math.sqrt(2.0 / math.pi)
    return 0.5 * x * (1.0 + jnp.tanh(c * (x + 0.044715 * (x * x * x))))


def _sgu_kernel(c_ref, ng_ref, nb_ref, w_ref, bias_ref, bng_ref, y_ref):
    tm = c_ref.shape[1]
    u = _gelu(c_ref[0, :, 0:SGU_WIDTH])
    v = _gelu(c_ref[0, :, SGU_WIDTH:2 * SGU_WIDTH])
    gc = c_ref[0, :, 2 * SGU_WIDTH:3 * SGU_WIDTH]
    mu = jnp.mean(v, axis=-1, keepdims=True)
    vc = v - mu
    var = jnp.mean(vc * vc, axis=-1, keepdims=True)
    vn = (vc * lax.rsqrt(var + EPS) * ng_ref[...] + nb_ref[...]).astype(BF16)

    w = w_ref[...]
    wi = (lax.broadcasted_iota(jnp.int32, w.shape, 0) % SGU_BLOCK) // CHUNK
    wj = lax.broadcasted_iota(jnp.int32, w.shape, 1) // CHUNK
    w = jnp.where(wi >= wj, w, 0.0).astype(BF16)

    lane_group = lax.broadcasted_iota(jnp.int32, (SGU_BLOCK, SGU_WIDTH), 1) // SGU_GROUP_DIM
    mixed = []
    for n in range(tm // SGU_BLOCK):
        r = jnp.dot(w, vn[n * SGU_BLOCK:(n + 1) * SGU_BLOCK, :], preferred_element_type=F32)
        m = r[0:SGU_BLOCK, :]
        for g in range(1, SGU_GROUPS):
            m = jnp.where(lane_group == g, r[g * SGU_BLOCK:(g + 1) * SGU_BLOCK, :], m)
        mixed.append(m + bias_ref[...])
    mixed = jnp.concatenate(mixed, axis=0)

    yc = u * mixed * _silu(gc)
    ms = jnp.mean(yc * yc, axis=-1, keepdims=True)
    y_ref[0] = (yc * lax.rsqrt(ms + EPS) * bng_ref[...]).astype(y_ref.dtype)


def _sgu(c_in, n_g, n_b, w_stack, bias, bn_g, tm):
    B, S, _ = c_in.shape
    full = lambda shape: pl.BlockSpec(shape, lambda b, i: (0,) * len(shape))
    return pl.pallas_call(
        _sgu_kernel,
        grid=(B, S // tm),
        in_specs=[
            pl.BlockSpec((1, tm, 3 * SGU_WIDTH), lambda b, i: (b, i, 0)),
            full((1, SGU_WIDTH)),
            full((1, SGU_WIDTH)),
            full((SGU_GROUPS * SGU_BLOCK, SGU_BLOCK)),
            full((SGU_BLOCK, SGU_WIDTH)),
            full((1, SGU_WIDTH)),
        ],
        out_specs=pl.BlockSpec((1, tm, SGU_WIDTH), lambda b, i: (b, i, 0)),
        out_shape=jax.ShapeDtypeStruct((B, S, SGU_WIDTH), BF16),
        compiler_params=_cparams("parallel", "parallel"),
        name="sgu",
    )(c_in, n_g, n_b, w_stack, bias, bn_g)


def _outproj_kernel(x_ref, ya_ref, ot_ref, gbt_ref, yc_ref, bngb_ref, w1_ref, w2_ref, w3_ref,
                    pg_ref, out_ref):
    ybt = ot_ref[0] * _silu(gbt_ref[0])
    ms = jnp.mean(ybt * ybt, axis=0, keepdims=True)
    ybt = (ybt * lax.rsqrt(ms + EPS) * bngb_ref[...]).astype(BF16)
    y = jnp.dot(ya_ref[0], w1_ref[...], preferred_element_type=F32)
    y = y + lax.dot_general(ybt, w2_ref[...], TN_DIMS, preferred_element_type=F32)
    y = y + jnp.dot(yc_ref[0], w3_ref[...], preferred_element_type=F32)
    ms = jnp.mean(y * y, axis=-1, keepdims=True)
    out_ref[0] = x_ref[0] + y * lax.rsqrt(ms + EPS) * pg_ref[...]


def _outproj(x, ya, ot, gbt, yc, bng_b, w1, w2, w3, post_g, tm):
    B, S, D = x.shape
    full = lambda shape: pl.BlockSpec(shape, lambda b, i: (0,) * len(shape))
    return pl.pallas_call(
        _outproj_kernel,
        grid=(B, S // tm),
        in_specs=[
            pl.BlockSpec((1, tm, D), lambda b, i: (b, i, 0)),
            pl.BlockSpec((1, tm, LRU_WIDTH), lambda b, i: (b, i, 0)),
            pl.BlockSpec((1, MLA_WIDTH, tm), lambda b, i: (b, 0, i)),
            pl.BlockSpec((1, MLA_WIDTH, tm), lambda b, i: (b, 0, i)),
            pl.BlockSpec((1, tm, SGU_WIDTH), lambda b, i: (b, i, 0)),
            full((MLA_WIDTH, 1)),
            full((LRU_WIDTH, D)),
            full((MLA_WIDTH, D)),
            full((SGU_WIDTH, D)),
            full((1, D)),
        ],
        out_specs=pl.BlockSpec((1, tm, D), lambda b, i: (b, i, 0)),
        out_shape=jax.ShapeDtypeStruct((B, S, D), F32),
        compiler_params=_cparams("parallel", "parallel"),
        name="outproj",
    )(x, ya, ot, gbt, yc, bng_b, w1, w2, w3, post_g)


def _layer_params(l, pre_norm_g, w_in, conv_w, conv_b, lru_wa, lru_ba, lru_wx, lru_bx, lru_lambda,
                  q_norm_g, w_uq, kv_norm_g, w_ukv, sgu_norm_g, sgu_norm_b, sgu_w, sgu_b,
                  branch_norm_g, w_out, post_norm_g):
    D = w_in.shape[1]
    offs = np.cumsum([0, LRU_WIDTH, LRU_WIDTH, Q_RANK, KV_RANK, MLA_ROPE, MLA_WIDTH,
                      SGU_WIDTH, SGU_WIDTH, SGU_WIDTH])
    win = w_in[l]
    cols = lambda i: win[:, offs[i]:offs[i + 1]]
    zeros = lambda n: jnp.zeros((D, n), win.dtype)
    w_kr = cols(4)
    half = MLA_ROPE // 2
    w_kr_sw = jnp.concatenate([w_kr[:, half:], w_kr[:, :half]], axis=1)
    place = lambda w: jnp.concatenate([zeros(MLA_NOPE), w, zeros(LANES - MLA_QK)], axis=1)
    w_tok = jnp.concatenate([
        cols(0), cols(1),
        cols(2), zeros(Q_LAT_PAD - Q_RANK), cols(3),
        place(w_kr), place(w_kr_sw),
        cols(6), cols(7), cols(8),
    ], axis=1).astype(BF16)
    w_gbt = cols(5).T.astype(BF16)

    def block_diag(w):
        eye = jnp.eye(LRU_HEADS, dtype=w.dtype)
        return jnp.einsum('hij,hg->higj', w, eye).reshape(LRU_WIDTH, LRU_WIDTH)

    w_gate = jnp.concatenate([block_diag(lru_wa[l]), block_diag(lru_wx[l])], axis=1).astype(BF16)
    b_gate = jnp.concatenate([lru_ba[l], lru_bx[l]]).reshape(1, -1)

    q_g = jnp.concatenate([q_norm_g[l], jnp.zeros((Q_LAT_PAD - Q_RANK,), F32)]).reshape(1, -1)
    wqt = jnp.concatenate([w_uq[l], jnp.zeros((Q_LAT_PAD - Q_RANK, w_uq.shape[2]), F32)], axis=0)
    wqt = wqt.T.astype(BF16)
    wkv = w_ukv[l].reshape(KV_RANK, MLA_HEADS, MLA_NOPE + MLA_V)
    wk = jnp.concatenate([wkv[:, :, :MLA_NOPE],
                          jnp.zeros((KV_RANK, MLA_HEADS, LANES - MLA_NOPE), F32)], axis=2)
    wk = wk.reshape(KV_RANK, MLA_HEADS * LANES).astype(BF16)
    wvt = wkv[:, :, MLA_NOPE:].reshape(KV_RANK, MLA_HEADS * MLA_V).T.astype(BF16)

    w_stack = sgu_w[l].reshape(SGU_GROUPS * SGU_BLOCK, SGU_BLOCK)
    sgu_bias = jnp.repeat(sgu_b[l].T, SGU_GROUP_DIM, axis=1)

    bng = branch_norm_g[l]
    wo = w_out[l].astype(BF16)
    return dict(
        pre_g=pre_norm_g[l].reshape(1, -1), w_tok=w_tok, w_gbt=w_gbt,
        conv_w=conv_w[l], conv_b=conv_b[l].reshape(1, -1), w_gate=w_gate, b_gate=b_gate,
        lam=lru_lambda[l].reshape(1, -1),
        q_g=q_g, kv_g=kv_norm_g[l].reshape(1, -1), wqt=wqt, wk=wk, wvt=wvt,
        sgu_g=sgu_norm_g[l].reshape(1, -1), sgu_nb=sgu_norm_b[l].reshape(1, -1),
        w_stack=w_stack, sgu_bias=sgu_bias,
        bng_a=bng[:LRU_WIDTH].reshape(1, -1),
        bng_b=bng[LRU_WIDTH:LRU_WIDTH + MLA_WIDTH].reshape(-1, 1),
        bng_c=bng[LRU_WIDTH + MLA_WIDTH:].reshape(1, -1),
        w1=wo[:LRU_WIDTH], w2=wo[LRU_WIDTH:LRU_WIDTH + MLA_WIDTH], w3=wo[LRU_WIDTH + MLA_WIDTH:],
        post_g=post_norm_g[l].reshape(1, -1),
    )


def kernel(x, positions, pre_norm_g, w_in, conv_w, conv_b, lru_wa, lru_ba, lru_wx, lru_bx, lru_lambda,
           q_norm_g, w_uq, kv_norm_g, w_ukv, sgu_norm_g, sgu_norm_b, sgu_w, sgu_b, branch_norm_g,
           w_out, post_norm_g):
    B, S, D = x.shape
    depth = w_in.shape[0]
    tm = min(ROW_TILE, S)
    tq = min(ATT_TQ, S)
    assert S % tm == 0 and S % tq == 0 and tq % CHUNK == 0 and tm % SGU_BLOCK == 0

    ct, st, c128, s128 = _rope_tables(positions, tm)
    h = x
    for l in range(depth):
        p = _layer_params(l, pre_norm_g, w_in, conv_w, conv_b, lru_wa, lru_ba, lru_wx, lru_bx,
                          lru_lambda, q_norm_g, w_uq, kv_norm_g, w_ukv, sgu_norm_g, sgu_norm_b,
                          sgu_w, sgu_b, branch_norm_g, w_out, post_norm_g)
        a_in, lat, kr, c_in, gbt = _inproj(h, p['pre_g'], p['w_tok'], p['w_gbt'], tm)
        ya = _rglru(a_in, p['conv_w'], p['conv_b'], p['w_gate'], p['b_gate'], p['lam'], p['bng_a'], tm)
        qt, k, vt = _mla_prep(lat, kr, ct, st, c128, s128, p['q_g'], p['kv_g'],
                              p['wqt'], p['wk'], p['wvt'], tm)
        ot = _attention(qt, k, vt, tq).reshape(B, MLA_WIDTH, S)
        yc = _sgu(c_in, p['sgu_g'], p['sgu_nb'], p['w_stack'], p['sgu_bias'], p['bng_c'], tm)
        h = _outproj(h, ya, ot, gbt, yc, p['bng_b'], p['w1'], p['w2'], p['w3'], p['post_g'], tm)
    return h
```

```python
import functools
import math

import numpy as np
import jax
import jax.numpy as jnp
from jax import lax
from jax.experimental import pallas as pl
from jax.experimental.pallas import tpu as pltpu

F32 = jnp.float32
BF16 = jnp.bfloat16

EPS = 1e-6
CHUNK = 64

LRU_WIDTH = 384
LRU_HEADS = 6
LRU_HEAD_DIM = 64
CONV_WIDTH = 4
LRU_C = 8.0

MLA_HEADS = 6
MLA_NOPE = 64
MLA_ROPE = 32
MLA_V = 64
MLA_QK = MLA_NOPE + MLA_ROPE
MLA_WIDTH = MLA_HEADS * MLA_V
Q_RANK = 192
KV_RANK = 128
ROPE_THETA = 10000.0

SGU_WIDTH = 256
SGU_GROUPS = 4
SGU_GROUP_DIM = 64
SGU_BLOCK = 128

LANES = 128
QK_PAD = 128
Q_LAT_PAD = 256

COL_A = 0
COL_LAT = COL_A + 2 * LRU_WIDTH
COL_KR = COL_LAT + Q_LAT_PAD + KV_RANK
COL_C = COL_KR + 2 * LANES
N_TOK = COL_C + 3 * SGU_WIDTH

ROW_TILE = 512
ATT_TQ = 512
ATT_KBIG = 4
VMEM_LIMIT = 48 * 1024 * 1024
ATT_VMEM_LIMIT = 56 * 1024 * 1024

NT_DIMS = (((1,), (1,)), ((), ()))
TN_DIMS = (((0,), (0,)), ((), ()))


def _cparams(*sem):
    return pltpu.CompilerParams(dimension_semantics=sem, vmem_limit_bytes=VMEM_LIMIT)


def _silu(x):
    return x * jax.nn.sigmoid(x)


def _rope_table_kernel(pos_ref, inv_ref, ct_ref, st_ref, c128_ref, s128_ref):
    pos = pos_ref[0].astype(F32)
    ang = inv_ref[...] * pos
    cos = jnp.cos(ang)
    sin = jnp.sin(ang)
    row = lax.broadcasted_iota(jnp.int32, ang.shape, 0)
    sin_signed = jnp.where(row < MLA_ROPE // 2, -sin, sin)
    ct_ref[0] = cos
    st_ref[0] = sin_signed
    ts = ang.shape[1]
    zeros_lo = jnp.zeros((MLA_NOPE, ts), F32)
    zeros_hi = jnp.zeros((LANES - MLA_QK, ts), F32)
    c128_ref[0] = jnp.concatenate([zeros_lo, cos, zeros_hi], axis=0).T
    s128_ref[0] = jnp.concatenate([zeros_lo, sin_signed, zeros_hi], axis=0).T


def _rope_tables(positions, ts):
    B, S = positions.shape
    half = MLA_ROPE // 2
    inv_freq = ROPE_THETA ** (-jnp.arange(half, dtype=F32) / half)
    inv2 = jnp.concatenate([inv_freq, inv_freq]).reshape(MLA_ROPE, 1)
    pos3 = positions.reshape(B, 1, S)
    return pl.pallas_call(
        _rope_table_kernel,
        grid=(B, S // ts),
        in_specs=[
            pl.BlockSpec((1, 1, ts), lambda b, i: (b, 0, i)),
            pl.BlockSpec((MLA_ROPE, 1), lambda b, i: (0, 0)),
        ],
        out_specs=[
            pl.BlockSpec((1, MLA_ROPE, ts), lambda b, i: (b, 0, i)),
            pl.BlockSpec((1, MLA_ROPE, ts), lambda b, i: (b, 0, i)),
            pl.BlockSpec((1, ts, LANES), lambda b, i: (b, i, 0)),
            pl.BlockSpec((1, ts, LANES), lambda b, i: (b, i, 0)),
        ],
        out_shape=[
            jax.ShapeDtypeStruct((B, MLA_ROPE, S), F32),
            jax.ShapeDtypeStruct((B, MLA_ROPE, S), F32),
            jax.ShapeDtypeStruct((B, S, LANES), F32),
            jax.ShapeDtypeStruct((B, S, LANES), F32),
        ],
        compiler_params=_cparams("parallel", "parallel"),
        name="rope_tables",
    )(pos3, inv2)


def _inproj_kernel(x_ref, g_ref, wtok_ref, wgbt_ref, a_ref, lat_ref, kr_ref, c_ref, gbt_ref):
    x = x_ref[0]
    ms = jnp.mean(x * x, axis=-1, keepdims=True)
    h = (x * lax.rsqrt(ms + EPS) * g_ref[...]).astype(BF16)

    def proj(lo, hi):
        return jnp.dot(h, wtok_ref[:, lo:hi], preferred_element_type=F32)

    a_ref[0] = proj(COL_A, COL_LAT)
    lat_ref[0] = proj(COL_LAT, COL_KR)
    kr_ref[0] = proj(COL_KR, COL_C)
    c_ref[0] = proj(COL_C, N_TOK)
    gbt_ref[0] = lax.dot_general(wgbt_ref[...], h, NT_DIMS, preferred_element_type=F32)


def _inproj(x, pre_g, w_tok, w_gbt, tm):
    B, S, D = x.shape
    n_lat = COL_KR - COL_LAT
    return pl.pallas_call(
        _inproj_kernel,
        grid=(B, S // tm),
        in_specs=[
            pl.BlockSpec((1, tm, D), lambda b, i: (b, i, 0)),
            pl.BlockSpec((1, D), lambda b, i: (0, 0)),
            pl.BlockSpec((D, N_TOK), lambda b, i: (0, 0)),
            pl.BlockSpec((MLA_WIDTH, D), lambda b, i: (0, 0)),
        ],
        out_specs=[
            pl.BlockSpec((1, tm, 2 * LRU_WIDTH), lambda b, i: (b, i, 0)),
            pl.BlockSpec((1, tm, n_lat), lambda b, i: (b, i, 0)),
            pl.BlockSpec((1, tm, 2 * LANES), lambda b, i: (b, i, 0)),
            pl.BlockSpec((1, tm, 3 * SGU_WIDTH), lambda b, i: (b, i, 0)),
            pl.BlockSpec((1, MLA_WIDTH, tm), lambda b, i: (b, 0, i)),
        ],
        out_shape=[
            jax.ShapeDtypeStruct((B, S, 2 * LRU_WIDTH), F32),
            jax.ShapeDtypeStruct((B, S, n_lat), F32),
            jax.ShapeDtypeStruct((B, S, 2 * LANES), F32),
            jax.ShapeDtypeStruct((B, S, 3 * SGU_WIDTH), F32),
            jax.ShapeDtypeStruct((B, MLA_WIDTH, S), F32),
        ],
        compiler_params=_cparams("parallel", "parallel"),
        name="inproj",
    )(x, pre_g, w_tok, w_gbt)


def _scan8(a, b):
    row = lax.broadcasted_iota(jnp.int32, a.shape, 0)
    for d in (1, 2, 4):
        a_prev = pltpu.roll(a, d, 0)
        b_prev = pltpu.roll(b, d, 0)
        live = row >= d
        b = jnp.where(live, a * b_prev + b, b)
        a = jnp.where(live, a * a_prev, a)
    return a, b


def _rglru_kernel(a_ref, convw_ref, convb_ref, wg_ref, bg_ref, lam_ref, bng_ref, y_ref,
                  xbuf, hcar, abuf, bbuf):
    ts = a_ref.shape[1]
    halo = 8

    @pl.when(pl.program_id(1) == 0)
    def _():
        xbuf[0:halo, :] = jnp.zeros((halo, LRU_WIDTH), F32)
        hcar[...] = jnp.zeros_like(hcar)

    xa = a_ref[0, :, 0:LRU_WIDTH]
    ga = a_ref[0, :, LRU_WIDTH:2 * LRU_WIDTH]
    xbuf[halo:halo + ts, :] = xa
    xc = convb_ref[...]
    for k in range(CONV_WIDTH):
        off = halo - (CONV_WIDTH - 1) + k
        xc = xc + convw_ref[k:k + 1, :] * xbuf[off:off + ts, :]
    xbuf[0:halo, :] = xbuf[ts:ts + halo, :]

    gz = jnp.dot(xc.astype(BF16), wg_ref[...], preferred_element_type=F32) + bg_ref[...]
    gate_a = jax.nn.sigmoid(gz[:, 0:LRU_WIDTH])
    gate_x = jax.nn.sigmoid(gz[:, LRU_WIDTH:2 * LRU_WIDTH])
    nl = -lam_ref[...]
    softplus = jnp.maximum(nl, 0.0) + jnp.log(1.0 + jnp.exp(-jnp.abs(nl)))
    log_a = (-LRU_C) * gate_a * softplus
    a = jnp.exp(log_a)
    mult = jnp.sqrt(1.0 - a * a)
    abuf[...] = a
    bbuf[...] = mult * (gate_x * xc)

    def blk(i, h):
        r = pl.multiple_of(i * 8, 8)
        a8, b8 = _scan8(abuf[pl.ds(r, 8), :], bbuf[pl.ds(r, 8), :])
        rows = a8 * h + b8
        bbuf[pl.ds(r, 8), :] = rows
        return rows[7:8, :]

    hcar[...] = lax.fori_loop(0, ts // 8, blk, hcar[...])

    ya = bbuf[...] * _silu(ga)
    ms = jnp.mean(ya * ya, axis=-1, keepdims=True)
    y_ref[0] = (ya * lax.rsqrt(ms + EPS) * bng_ref[...]).astype(y_ref.dtype)


def _rglru(a_in, conv_w, conv_b, w_gate, b_gate, lam, bn_g, ts):
    B, S, _ = a_in.shape
    full = lambda shape: pl.BlockSpec(shape, lambda b, i: (0,) * len(shape))
    return pl.pallas_call(
        _rglru_kernel,
        grid=(B, S // ts),
        in_specs=[
            pl.BlockSpec((1, ts, 2 * LRU_WIDTH), lambda b, i: (b, i, 0)),
            full((CONV_WIDTH, LRU_WIDTH)),
            full((1, LRU_WIDTH)),
            full((LRU_WIDTH, 2 * LRU_WIDTH)),
            full((1, 2 * LRU_WIDTH)),
            full((1, LRU_WIDTH)),
            full((1, LRU_WIDTH)),
        ],
        out_specs=pl.BlockSpec((1, ts, LRU_WIDTH), lambda b, i: (b, i, 0)),
        out_shape=jax.ShapeDtypeStruct((B, S, LRU_WIDTH), BF16),
        scratch_shapes=[
            pltpu.VMEM((ts + 8, LRU_WIDTH), F32),
            pltpu.VMEM((1, LRU_WIDTH), F32),
            pltpu.VMEM((ts, LRU_WIDTH), F32),
            pltpu.VMEM((ts, LRU_WIDTH), F32),
        ],
        compiler_params=_cparams("parallel", "arbitrary"),
        name="rglru",
    )(a_in, conv_w, conv_b, w_gate, b_gate, lam, bn_g)


def _mla_prep_kernel(lat_ref, kr_ref, ct_ref, st_ref, c128_ref, s128_ref, qg_ref, kvg_ref,
                     wqt_ref, wk_ref, wvt_ref, qt_ref, k_ref, vt_ref):
    ts = lat_ref.shape[1]
    q_lat = lat_ref[0, :, 0:Q_LAT_PAD]
    kv_lat = lat_ref[0, :, Q_LAT_PAD:Q_LAT_PAD + KV_RANK]
    q_ms = jnp.sum(q_lat * q_lat, axis=-1, keepdims=True) * (1.0 / Q_RANK)
    qn = (q_lat * lax.rsqrt(q_ms + EPS) * qg_ref[...]).astype(BF16)
    kv_ms = jnp.mean(kv_lat * kv_lat, axis=-1, keepdims=True)
    kvn = (kv_lat * lax.rsqrt(kv_ms + EPS) * kvg_ref[...]).astype(BF16)

    qt = lax.dot_general(wqt_ref[...], qn, NT_DIMS, preferred_element_type=F32)
    ct = ct_ref[0]
    st = st_ref[0]
    half = MLA_ROPE // 2
    q_scale = (MLA_QK ** -0.5) * math.log2(math.e)
    pad = jnp.zeros((QK_PAD - MLA_QK, ts), F32)
    for h in range(MLA_HEADS):
        base = h * MLA_QK
        nope = qt[base:base + MLA_NOPE, :]
        pe = qt[base + MLA_NOPE:base + MLA_QK, :]
        pe_sw = jnp.concatenate([pe[half:, :], pe[:half, :]], axis=0)
        pe = pe * ct + pe_sw * st
        qh = jnp.concatenate([nope, pe, pad], axis=0) * q_scale
        qt_ref[0, h] = qh.astype(qt_ref.dtype)

    kn = jnp.dot(kvn, wk_ref[...], preferred_element_type=F32)
    pe128 = kr_ref[0, :, 0:LANES] * c128_ref[0] + kr_ref[0, :, LANES:2 * LANES] * s128_ref[0]
    for h in range(MLA_HEADS):
        k_ref[0, h] = (kn[:, h * LANES:(h + 1) * LANES] + pe128).astype(k_ref.dtype)

    vt = lax.dot_general(wvt_ref[...], kvn, NT_DIMS, preferred_element_type=F32)
    for h in range(MLA_HEADS):
        vt_ref[0, h] = vt[h * MLA_V:(h + 1) * MLA_V, :].astype(vt_ref.dtype)


def _mla_prep(lat, kr, ct, st, c128, s128, q_g, kv_g, wqt, wk, wvt, ts):
    B, S, n_lat = lat.shape
    full = lambda shape: pl.BlockSpec(shape, lambda b, i: (0,) * len(shape))
    return pl.pallas_call(
        _mla_prep_kernel,
        grid=(B, S // ts),
        in_specs=[
            pl.BlockSpec((1, ts, n_lat), lambda b, i: (b, i, 0)),
            pl.BlockSpec((1, ts, 2 * LANES), lambda b, i: (b, i, 0)),
            pl.BlockSpec((1, MLA_ROPE, ts), lambda b, i: (b, 0, i)),
            pl.BlockSpec((1, MLA_ROPE, ts), lambda b, i: (b, 0, i)),
            pl.BlockSpec((1, ts, LANES), lambda b, i: (b, i, 0)),
            pl.BlockSpec((1, ts, LANES), lambda b, i: (b, i, 0)),
            full((1, Q_LAT_PAD)),
            full((1, KV_RANK)),
            full((MLA_HEADS * MLA_QK, Q_LAT_PAD)),
            full((KV_RANK, MLA_HEADS * LANES)),
            full((MLA_HEADS * MLA_V, KV_RANK)),
        ],
        out_specs=[
            pl.BlockSpec((1, MLA_HEADS, QK_PAD, ts), lambda b, i: (b, 0, 0, i)),
            pl.BlockSpec((1, MLA_HEADS, ts, QK_PAD), lambda b, i: (b, 0, i, 0)),
            pl.BlockSpec((1, MLA_HEADS, MLA_V, ts), lambda b, i: (b, 0, 0, i)),
        ],
        out_shape=[
            jax.ShapeDtypeStruct((B, MLA_HEADS, QK_PAD, S), BF16),
            jax.ShapeDtypeStruct((B, MLA_HEADS, S, QK_PAD), BF16),
            jax.ShapeDtypeStruct((B, MLA_HEADS, MLA_V, S), BF16),
        ],
        compiler_params=_cparams("parallel", "parallel"),
        name="mla_prep",
    )(lat, kr, ct, st, c128, s128, q_g, kv_g, wqt, wk, wvt)


def _attn_kernel(qt_ref, k_ref, vt_ref, o_ref, s_even, s_odd, m_scr):
    i = pl.program_id(2)
    nq = pl.num_programs(2) - 1
    tq = qt_ref.shape[3]
    tk = tq
    sub = 8
    wslot = i % 2
    rslot = 1 - wslot

    @pl.when(i == 0)
    def _():
        m_scr[...] = jnp.zeros_like(m_scr)

    qt = qt_ref[0, 0]
    m_prev = m_scr[rslot]

    def fold(x):
        return x.reshape(x.shape[0] // sub, sub, tq)

    def scores(ks):
        k = k_ref[0, 0, pl.ds(ks, tk), :]
        return jnp.dot(k, qt, preferred_element_type=F32)

    def run(s_w, s_r):
        def score_tile(ks, mrun):
            s = scores(ks)
            s_w[pl.ds(ks, tk), :] = s
            return jnp.maximum(mrun, jnp.max(fold(s), axis=0))

        def finish_tile(ks, lrun, acc):
            p = jnp.exp2(s_r[pl.ds(ks, tk), :] - m_prev)
            lrun = lrun + jnp.sum(fold(p), axis=0)
            v = vt_ref[0, 0, :, pl.ds(ks, tk)]
            acc = acc + jnp.dot(v, p.astype(BF16), preferred_element_type=F32)
            return lrun, acc

        def both(nsub):
            def body(j, carry):
                mrun, lrun, acc = carry
                for u in range(nsub):
                    ks = pl.multiple_of((j * nsub + u) * tk, tk)
                    mrun = score_tile(ks, mrun)
                    lrun, acc = finish_tile(ks, lrun, acc)
                return mrun, lrun, acc
            return body

        def finish_only(nsub):
            def body(j, carry):
                lrun, acc = carry
                for u in range(nsub):
                    ks = pl.multiple_of((j * nsub + u) * tk, tk)
                    lrun, acc = finish_tile(ks, lrun, acc)
                return lrun, acc
            return body

        n_both = jnp.where(i < nq, i, 0)
        n_big = n_both // ATT_KBIG
        carry = (jnp.full((sub, tq), -jnp.inf, F32), jnp.zeros((sub, tq), F32),
                 jnp.zeros((MLA_V, tq), F32))
        carry = lax.fori_loop(0, n_big, both(ATT_KBIG), carry)
        mrun, lrun, acc = lax.fori_loop(n_big * ATT_KBIG, n_both, both(1), carry)
        n_only = i - n_both
        n_big = n_only // ATT_KBIG
        carry = lax.fori_loop(0, n_big, finish_only(ATT_KBIG), (lrun, acc))
        lrun, acc = lax.fori_loop(n_big * ATT_KBIG, n_only, finish_only(1), carry)

        @pl.when(i < nq)
        def _():
            ks = pl.multiple_of(i * tk, tk)
            s = scores(ks)
            kc = lax.broadcasted_iota(jnp.int32, s.shape, 0) // CHUNK
            qc = lax.broadcasted_iota(jnp.int32, s.shape, 1) // CHUNK
            s = jnp.where(kc <= qc, s, -jnp.inf)
            s_w[pl.ds(ks, tk), :] = s
            mtile = jnp.maximum(mrun, jnp.max(fold(s), axis=0))
            m_scr[wslot] = jnp.max(mtile, axis=0, keepdims=True)

        return lrun, acc

    lrun, acc = lax.cond(wslot == 0, lambda: run(s_even, s_odd), lambda: run(s_odd, s_even))

    @pl.when(i > 0)
    def _():
        l = jnp.sum(lrun, axis=0, keepdims=True)
        o_ref[0, 0] = (acc / l).astype(o_ref.dtype)


def _attention(qt, k, vt, tq):
    B, H, _, S = qt.shape
    nq = S // tq
    return pl.pallas_call(
        _attn_kernel,
        grid=(B, H, nq + 1),
        in_specs=[
            pl.BlockSpec((1, 1, QK_PAD, tq), lambda b, h, i: (b, h, 0, jnp.minimum(i, nq - 1))),
            pl.BlockSpec((1, 1, S, QK_PAD), lambda b, h, i: (b, h, 0, 0)),
            pl.BlockSpec((1, 1, MLA_V, S), lambda b, h, i: (b, h, 0, 0)),
        ],
        out_specs=pl.BlockSpec((1, 1, MLA_V, tq), lambda b, h, i: (b, h, 0, jnp.maximum(i - 1, 0))),
        out_shape=jax.ShapeDtypeStruct((B, H, MLA_V, S), F32),
        scratch_shapes=[pltpu.VMEM((S, tq), F32), pltpu.VMEM((S, tq), F32),
                        pltpu.VMEM((2, 1, tq), F32)],
        compiler_params=pltpu.CompilerParams(
            dimension_semantics=("parallel", "parallel", "arbitrary"),
            vmem_limit_bytes=ATT_VMEM_LIMIT),
        name="attention",
    )(qt, k, vt)


def _gelu(x):
    c = math.sqrt(2.0 / math.pi)
    return 0.5 * x * (1.0 + jnp.tanh(c * (x + 0.044715 * (x * x * x))))


def _sgu_kernel(c_ref, ng_ref, nb_ref, w_ref, bias_ref, bng_ref, y_ref):
    tm = c_ref.shape[1]
    u = _gelu(c_ref[0, :, 0:SGU_WIDTH])
    v = _gelu(c_ref[0, :, SGU_WIDTH:2 * SGU_WIDTH])
    gc = c_ref[0, :, 2 * SGU_WIDTH:3 * SGU_WIDTH]
    mu = jnp.mean(v, axis=-1, keepdims=True)
    vc = v - mu
    var = jnp.mean(vc * vc, axis=-1, keepdims=True)
    vn = (vc * lax.rsqrt(var + EPS) * ng_ref[...] + nb_ref[...]).astype(BF16)

    w = w_ref[...]
    wi = (lax.broadcasted_iota(jnp.int32, w.shape, 0) % SGU_BLOCK) // CHUNK
    wj = lax.broadcasted_iota(jnp.int32, w.shape, 1) // CHUNK
    w = jnp.where(wi >= wj, w, 0.0).astype(BF16)

    lane_group = lax.broadcasted_iota(jnp.int32, (SGU_BLOCK, SGU_WIDTH), 1) // SGU_GROUP_DIM
    mixed = []
    for n in range(tm // SGU_BLOCK):
        r = jnp.dot(w, vn[n * SGU_BLOCK:(n + 1) * SGU_BLOCK, :], preferred_element_type=F32)
        m = r[0:SGU_BLOCK, :]
        for g in range(1, SGU_GROUPS):
            m = jnp.where(lane_group == g, r[g * SGU_BLOCK:(g + 1) * SGU_BLOCK, :], m)
        mixed.append(m + bias_ref[...])
    mixed = jnp.concatenate(mixed, axis=0)

    yc = u * mixed * _silu(gc)
    ms = jnp.mean(yc * yc, axis=-1, keepdims=True)
    y_ref[0] = (yc * lax.rsqrt(ms + EPS) * bng_ref[...]).astype(y_ref.dtype)


def _sgu(c_in, n_g, n_b, w_stack, bias, bn_g, tm):
    B, S, _ = c_in.shape
    full = lambda shape: pl.BlockSpec(shape, lambda b, i: (0,) * len(shape))
    return pl.pallas_call(
        _sgu_kernel,
        grid=(B, S // tm),
        in_specs=[
            pl.BlockSpec((1, tm, 3 * SGU_WIDTH), lambda b, i: (b, i, 0)),
            full((1, SGU_WIDTH)),
            full((1, SGU_WIDTH)),
            full((SGU_GROUPS * SGU_BLOCK, SGU_BLOCK)),
            full((SGU_BLOCK, SGU_WIDTH)),
            full((1, SGU_WIDTH)),
        ],
        out_specs=pl.BlockSpec((1, tm, SGU_WIDTH), lambda b, i: (b, i, 0)),
        out_shape=jax.ShapeDtypeStruct((B, S, SGU_WIDTH), BF16),
        compiler_params=_cparams("parallel", "parallel"),
        name="sgu",
    )(c_in, n_g, n_b, w_stack, bias, bn_g)


def _outproj_kernel(x_ref, ya_ref, ot_ref, gbt_ref, yc_ref, bngb_ref, w1_ref, w2_ref, w3_ref,
                    pg_ref, out_ref):
    ybt = ot_ref[0] * _silu(gbt_ref[0])
    ms = jnp.mean(ybt * ybt, axis=0, keepdims=True)
    ybt = (ybt * lax.rsqrt(ms + EPS) * bngb_ref[...]).astype(BF16)
    y = jnp.dot(ya_ref[0], w1_ref[...], preferred_element_type=F32)
    y = y + lax.dot_general(ybt, w2_ref[...], TN_DIMS, preferred_element_type=F32)
    y = y + jnp.dot(yc_ref[0], w3_ref[...], preferred_element_type=F32)
    ms = jnp.mean(y * y, axis=-1, keepdims=True)
    out_ref[0] = x_ref[0] + y * lax.rsqrt(ms + EPS) * pg_ref[...]


def _outproj(x, ya, ot, gbt, yc, bng_b, w1, w2, w3, post_g, tm):
    B, S, D = x.shape
    full = lambda shape: pl.BlockSpec(shape, lambda b, i: (0,) * len(shape))
    return pl.pallas_call(
        _outproj_kernel,
        grid=(B, S // tm),
        in_specs=[
            pl.BlockSpec((1, tm, D), lambda b, i: (b, i, 0)),
            pl.BlockSpec((1, tm, LRU_WIDTH), lambda b, i: (b, i, 0)),
            pl.BlockSpec((1, MLA_WIDTH, tm), lambda b, i: (b, 0, i)),
            pl.BlockSpec((1, MLA_WIDTH, tm), lambda b, i: (b, 0, i)),
            pl.BlockSpec((1, tm, SGU_WIDTH), lambda b, i: (b, i, 0)),
            full((MLA_WIDTH, 1)),
            full((LRU_WIDTH, D)),
            full((MLA_WIDTH, D)),
            full((SGU_WIDTH, D)),
            full((1, D)),
        ],
        out_specs=pl.BlockSpec((1, tm, D), lambda b, i: (b, i, 0)),
        out_shape=jax.ShapeDtypeStruct((B, S, D), F32),
        compiler_params=_cparams("parallel", "parallel"),
        name="outproj",
    )(x, ya, ot, gbt, yc, bng_b, w1, w2, w3, post_g)


def _layer_params(l, pre_norm_g, w_in, conv_w, conv_b, lru_wa, lru_ba, lru_wx, lru_bx, lru_lambda,
                  q_norm_g, w_uq, kv_norm_g, w_ukv, sgu_norm_g, sgu_norm_b, sgu_w, sgu_b,
                  branch_norm_g, w_out, post_norm_g):
    D = w_in.shape[1]
    offs = np.cumsum([0, LRU_WIDTH, LRU_WIDTH, Q_RANK, KV_RANK, MLA_ROPE, MLA_WIDTH,
                      SGU_WIDTH, SGU_WIDTH, SGU_WIDTH])
    win = w_in[l]
    cols = lambda i: win[:, offs[i]:offs[i + 1]]
    zeros = lambda n: jnp.zeros((D, n), win.dtype)
    w_kr = cols(4)
    half = MLA_ROPE // 2
    w_kr_sw = jnp.concatenate([w_kr[:, half:], w_kr[:, :half]], axis=1)
    place = lambda w: jnp.concatenate([zeros(MLA_NOPE), w, zeros(LANES - MLA_QK)], axis=1)
    w_tok = jnp.concatenate([
        cols(0), cols(1),
        cols(2), zeros(Q_LAT_PAD - Q_RANK), cols(3),
        place(w_kr), place(w_kr_sw),
        cols(6), cols(7), cols(8),
    ], axis=1).astype(BF16)
    w_gbt = cols(5).T.astype(BF16)

    def block_diag(w):
        eye = jnp.eye(LRU_HEADS, dtype=w.dtype)
        return jnp.einsum('hij,hg->higj', w, eye).reshape(LRU_WIDTH, LRU_WIDTH)

    w_gate = jnp.concatenate([block_diag(lru_wa[l]), block_diag(lru_wx[l])], axis=1).astype(BF16)
    b_gate = jnp.concatenate([lru_ba[l], lru_bx[l]]).reshape(1, -1)

    q_g = jnp.concatenate([q_norm_g[l], jnp.zeros((Q_LAT_PAD - Q_RANK,), F32)]).reshape(1, -1)
    wqt = jnp.concatenate([w_uq[l], jnp.zeros((Q_LAT_PAD - Q_RANK, w_uq.shape[2]), F32)], axis=0)
    wqt = wqt.T.astype(BF16)
    wkv = w_ukv[l].reshape(KV_RANK, MLA_HEADS, MLA_NOPE + MLA_V)
    wk = jnp.concatenate([wkv[:, :, :MLA_NOPE],
                          jnp.zeros((KV_RANK, MLA_HEADS, LANES - MLA_NOPE), F32)], axis=2)
    wk = wk.reshape(KV_RANK, MLA_HEADS * LANES).astype(BF16)
    wvt = wkv[:, :, MLA_NOPE:].reshape(KV_RANK, MLA_HEADS * MLA_V).T.astype(BF16)

    w_stack = sgu_w[l].reshape(SGU_GROUPS * SGU_BLOCK, SGU_BLOCK)
    sgu_bias = jnp.repeat(sgu_b[l].T, SGU_GROUP_DIM, axis=1)

    bng = branch_norm_g[l]
    wo = w_out[l].astype(BF16)
    return dict(
        pre_g=pre_norm_g[l].reshape(1, -1), w_tok=w_tok, w_gbt=w_gbt,
        conv_w=conv_w[l], conv_b=conv_b[l].reshape(1, -1), w_gate=w_gate, b_gate=b_gate,
        lam=lru_lambda[l].reshape(1, -1),
        q_g=q_g, kv_g=kv_norm_g[l].reshape(1, -1), wqt=wqt, wk=wk, wvt=wvt,
        sgu_g=sgu_norm_g[l].reshape(1, -1), sgu_nb=sgu_norm_b[l].reshape(1, -1),
        w_stack=w_stack, sgu_bias=sgu_bias,
        bng_a=bng[:LRU_WIDTH].reshape(1, -1),
        bng_b=bng[LRU_WIDTH:LRU_WIDTH + MLA_WIDTH].reshape(-1, 1),
        bng_c=bng[LRU_WIDTH + MLA_WIDTH:].reshape(1, -1),
        w1=wo[:LRU_WIDTH], w2=wo[LRU_WIDTH:LRU_WIDTH + MLA_WIDTH], w3=wo[LRU_WIDTH + MLA_WIDTH:],
        post_g=post_norm_g[l].reshape(1, -1),
    )


def kernel(x, positions, pre_norm_g, w_in, conv_w, conv_b, lru_wa, lru_ba, lru_wx, lru_bx, lru_lambda,
           q_norm_g, w_uq, kv_norm_g, w_ukv, sgu_norm_g, sgu_norm_b, sgu_w, sgu_b, branch_norm_g,
           w_out, post_norm_g):
    B, S, D = x.shape
    depth = w_in.shape[0]
    tm = min(ROW_TILE, S)
    tq = min(ATT_TQ, S)
    assert S % tm == 0 and S % tq == 0 and tq % CHUNK == 0 and tm % SGU_BLOCK == 0

    ct, st, c128, s128 = _rope_tables(positions, tm)
    h = x
    for l in range(depth):
        p = _layer_params(l, pre_norm_g, w_in, conv_w, conv_b, lru_wa, lru_ba, lru_wx, lru_bx,
                          lru_lambda, q_norm_g, w_uq, kv_norm_g, w_ukv, sgu_norm_g, sgu_norm_b,
                          sgu_w, sgu_b, branch_norm_g, w_out, post_norm_g)
        a_in, lat, kr, c_in, gbt = _inproj(h, p['pre_g'], p['w_tok'], p['w_gbt'], tm)
        ya = _rglru(a_in, p['conv_w'], p['conv_b'], p['w_gate'], p['b_gate'], p['lam'], p['bng_a'], tm)
        qt, k, vt = _mla_prep(lat, kr, ct, st, c128, s128, p['q_g'], p['kv_g'],
                              p['wqt'], p['wk'], p['wvt'], tm)
        ot = _attention(qt, k, vt, tq).reshape(B, MLA_WIDTH, S)
        yc = _sgu(c_in, p['sgu_g'], p['sgu_nb'], p['w_stack'], p['sgu_bias'], p['bng_c'], tm)
        h = _outproj(h, ya, ot, gbt, yc, p['bng_b'], p['w1'], p['w2'], p['w3'], p['post_g'], tm)
    return h
```

```python
import math

import numpy as np
import jax
import jax.numpy as jnp
from jax import lax
from jax.experimental import pallas as pl
from jax.experimental.pallas import tpu as pltpu

F32 = jnp.float32
BF16 = jnp.bfloat16

EPS = 1e-6
CHUNK = 64

LRU_WIDTH = 384
LRU_HEADS = 6
LRU_HEAD_DIM = 64
CONV_WIDTH = 4
LRU_C = 8.0

MLA_HEADS = 6
MLA_NOPE = 64
MLA_ROPE = 32
MLA_V = 64
MLA_QK = MLA_NOPE + MLA_ROPE
MLA_WIDTH = MLA_HEADS * MLA_V
Q_RANK = 192
KV_RANK = 128
ROPE_THETA = 10000.0

SGU_WIDTH = 256
SGU_GROUPS = 4
SGU_GROUP_DIM = 64
SGU_BLOCK = 128

LANES = 128
SUBLANES = 8
QK_PAD = 128
Q_LAT_PAD = 256

COL_A = 0
COL_LAT = COL_A + 2 * LRU_WIDTH
COL_C = COL_LAT + Q_LAT_PAD + KV_RANK
N_TOK = COL_C + 3 * SGU_WIDTH
assert (Q_RANK - LANES) == MLA_NOPE and Q_RANK + MLA_ROPE <= Q_LAT_PAD

ROW_TILE = 512
ATT_TQ = 512
ATT_KBIG = 4
VMEM_LIMIT = 48 * 1024 * 1024
ATT_VMEM_LIMIT = 56 * 1024 * 1024

NT_DIMS = (((1,), (1,)), ((), ()))
TN_DIMS = (((0,), (0,)), ((), ()))


def _cparams(*sem):
    return pltpu.CompilerParams(dimension_semantics=sem, vmem_limit_bytes=VMEM_LIMIT)


def _silu(x):
    return x * jax.nn.sigmoid(x)


def _gelu(x):
    c = math.sqrt(2.0 / math.pi)
    return 0.5 * x * (1.0 + jnp.tanh(c * (x + 0.044715 * (x * x * x))))


def _rope_table_kernel(pos_ref, inv_ref, ct_ref, st_ref, c128_ref, s128_ref):
    pos = pos_ref[0].astype(F32)
    ang = inv_ref[...] * pos
    cos = jnp.cos(ang)
    sin = jnp.sin(ang)
    row = lax.broadcasted_iota(jnp.int32, ang.shape, 0)
    sin_signed = jnp.where(row < MLA_ROPE // 2, -sin, sin)
    ct_ref[0] = cos
    st_ref[0] = sin_signed
    ts = ang.shape[1]
    zeros_lo = jnp.zeros((MLA_NOPE, ts), F32)
    zeros_hi = jnp.zeros((LANES - MLA_QK, ts), F32)
    c128_ref[0] = jnp.concatenate([zeros_lo, cos, zeros_hi], axis=0).T
    s128_ref[0] = jnp.concatenate([zeros_lo, sin_signed, zeros_hi], axis=0).T


def _rope_tables(positions, ts):
    B, S = positions.shape
    half = MLA_ROPE // 2
    inv_freq = ROPE_THETA ** (-jnp.arange(half, dtype=F32) / half)
    inv2 = jnp.concatenate([inv_freq, inv_freq]).reshape(MLA_ROPE, 1)
    pos3 = positions.reshape(B, 1, S)
    return pl.pallas_call(
        _rope_table_kernel,
        grid=(B, S // ts),
        in_specs=[
            pl.BlockSpec((1, 1, ts), lambda b, i: (b, 0, i)),
            pl.BlockSpec((MLA_ROPE, 1), lambda b, i: (0, 0)),
        ],
        out_specs=[
            pl.BlockSpec((1, MLA_ROPE, ts), lambda b, i: (b, 0, i)),
            pl.BlockSpec((1, MLA_ROPE, ts), lambda b, i: (b, 0, i)),
            pl.BlockSpec((1, ts, LANES), lambda b, i: (b, i, 0)),
            pl.BlockSpec((1, ts, LANES), lambda b, i: (b, i, 0)),
        ],
        out_shape=[
            jax.ShapeDtypeStruct((B, MLA_ROPE, S), F32),
            jax.ShapeDtypeStruct((B, MLA_ROPE, S), F32),
            jax.ShapeDtypeStruct((B, S, LANES), F32),
            jax.ShapeDtypeStruct((B, S, LANES), F32),
        ],
        compiler_params=_cparams("parallel", "parallel"),
        name="rope_tables",
    )(pos3, inv2)


def _scan8(a, b):
    row = lax.broadcasted_iota(jnp.int32, a.shape, 0)
    for d in (1, 2, 4):
        a_prev = pltpu.roll(a, d, 0)
        b_prev = pltpu.roll(b, d, 0)
        live = row >= d
        b = jnp.where(live, a * b_prev + b, b)
        a = jnp.where(live, a * a_prev, a)
    return a, b


def _rglru_branch(a_in, convw_ref, convb_ref, wg_ref, bg_ref, lam_ref, bng_ref,
                  xbuf, hcar, abuf, bbuf):
    ts = a_in.shape[0]
    halo = SUBLANES
    xa = a_in[:, 0:LRU_WIDTH]
    ga = a_in[:, LRU_WIDTH:2 * LRU_WIDTH]
    xbuf[halo:halo + ts, :] = xa
    xc = convb_ref[...]
    for k in range(CONV_WIDTH):
        off = halo - (CONV_WIDTH - 1) + k
        xc = xc + convw_ref[k:k + 1, :] * xbuf[off:off + ts, :]
    xbuf[0:halo, :] = xbuf[ts:ts + halo, :]

    gz = jnp.dot(xc.astype(BF16), wg_ref[...], preferred_element_type=F32) + bg_ref[...]
    gate_a = jax.nn.sigmoid(gz[:, 0:LRU_WIDTH])
    gate_x = jax.nn.sigmoid(gz[:, LRU_WIDTH:2 * LRU_WIDTH])
    nl = -lam_ref[...]
    softplus = jnp.maximum(nl, 0.0) + jnp.log(1.0 + jnp.exp(-jnp.abs(nl)))
    log_a = (-LRU_C) * gate_a * softplus
    a = jnp.exp(log_a)
    mult = jnp.sqrt(1.0 - a * a)
    abuf[...] = a
    bbuf[...] = mult * (gate_x * xc)

    def blk(i, h):
        r = pl.multiple_of(i * SUBLANES, SUBLANES)
        a8, b8 = _scan8(abuf[pl.ds(r, SUBLANES), :], bbuf[pl.ds(r, SUBLANES), :])
        rows = a8 * h + b8
        bbuf[pl.ds(r, SUBLANES), :] = rows
        return rows[SUBLANES - 1:SUBLANES, :]

    hcar[...] = lax.fori_loop(0, ts // SUBLANES, blk, hcar[...])

    ya = bbuf[...] * _silu(ga)
    ms = jnp.mean(ya * ya, axis=-1, keepdims=True)
    return ya * lax.rsqrt(ms + EPS) * bng_ref[...]


def _mla_prep(lat, ct, st, c128, s128, qg_ref, kvg_ref, wqt_ref, wk_ref, wvt_ref,
              qt_ref, k_ref, vt_ref):
    ts = lat.shape[0]
    q_blk = lat[:, 0:Q_LAT_PAD]
    kv_lat = lat[:, Q_LAT_PAD:Q_LAT_PAD + KV_RANK]
    q_lane = lax.broadcasted_iota(jnp.int32, q_blk.shape, 1)
    q_only = jnp.where(q_lane < Q_RANK, q_blk, 0.0)
    q_ms = jnp.sum(q_only * q_only, axis=-1, keepdims=True) * (1.0 / Q_RANK)
    qn = (q_blk * lax.rsqrt(q_ms + EPS) * qg_ref[...]).astype(BF16)
    kv_ms = jnp.mean(kv_lat * kv_lat, axis=-1, keepdims=True)
    kvn = (kv_lat * lax.rsqrt(kv_ms + EPS) * kvg_ref[...]).astype(BF16)

    qt = lax.dot_general(wqt_ref[...], qn, NT_DIMS, preferred_element_type=F32)
    half = MLA_ROPE // 2
    q_scale = (MLA_QK ** -0.5) * math.log2(math.e)
    pad = jnp.zeros((QK_PAD - MLA_QK, ts), F32)
    for h in range(MLA_HEADS):
        base = h * MLA_QK
        nope = qt[base:base + MLA_NOPE, :]
        pe = qt[base + MLA_NOPE:base + MLA_QK, :]
        pe_sw = jnp.concatenate([pe[half:, :], pe[:half, :]], axis=0)
        pe = pe * ct + pe_sw * st
        qh = jnp.concatenate([nope, pe, pad], axis=0) * q_scale
        qt_ref[0, h] = qh.astype(qt_ref.dtype)

    kn = jnp.dot(kvn, wk_ref[...], preferred_element_type=F32)
    kr = lat[:, LANES:2 * LANES]
    half_swapped = jnp.where(lax.broadcasted_iota(jnp.int32, kr.shape, 1) < MLA_NOPE + half,
                             pltpu.roll(kr, LANES - half, 1), pltpu.roll(kr, half, 1))
    pe128 = kr * c128 + half_swapped * s128
    for h in range(MLA_HEADS):
        k_ref[0, h] = (kn[:, h * LANES:(h + 1) * LANES] + pe128).astype(k_ref.dtype)

    vt = lax.dot_general(wvt_ref[...], kvn, NT_DIMS, preferred_element_type=F32)
    for h in range(MLA_HEADS):
        vt_ref[0, h] = vt[h * MLA_V:(h + 1) * MLA_V, :].astype(vt_ref.dtype)


def _sgu_branch(c_in, ng_ref, nb_ref, w_ref, bias_ref, bng_ref):
    tm = c_in.shape[0]
    u = _gelu(c_in[:, 0:SGU_WIDTH])
    v = _gelu(c_in[:, SGU_WIDTH:2 * SGU_WIDTH])
    gc = c_in[:, 2 * SGU_WIDTH:3 * SGU_WIDTH]
    mu = jnp.mean(v, axis=-1, keepdims=True)
    vc = v - mu
    var = jnp.mean(vc * vc, axis=-1, keepdims=True)
    vn = (vc * lax.rsqrt(var + EPS) * ng_ref[...] + nb_ref[...]).astype(BF16)

    w = w_ref[...]
    wi = (lax.broadcasted_iota(jnp.int32, w.shape, 0) % SGU_BLOCK) // CHUNK
    wj = lax.broadcasted_iota(jnp.int32, w.shape, 1) // CHUNK
    w = jnp.where(wi >= wj, w, 0.0).astype(BF16)

    lane_group = lax.broadcasted_iota(jnp.int32, (SGU_BLOCK, SGU_WIDTH), 1) // SGU_GROUP_DIM
    mixed = []
    for n in range(tm // SGU_BLOCK):
        r = jnp.dot(w, vn[n * SGU_BLOCK:(n + 1) * SGU_BLOCK, :], preferred_element_type=F32)
        m = r[0:SGU_BLOCK, :]
        for g in range(1, SGU_GROUPS):
            m = jnp.where(lane_group == g, r[g * SGU_BLOCK:(g + 1) * SGU_BLOCK, :], m)
        mixed.append(m + bias_ref[...])
    mixed = jnp.concatenate(mixed, axis=0)

    yc = u * mixed * _silu(gc)
    ms = jnp.mean(yc * yc, axis=-1, keepdims=True)
    return yc * lax.rsqrt(ms + EPS) * bng_ref[...]


def _front_kernel(x_ref, ct_ref, st_ref, c128_ref, s128_ref,
                  preg_ref, wtok_ref, wgbt_ref,
                  convw_ref, convb_ref, wg_ref, bg_ref, lam_ref, bnga_ref,
                  qg_ref, kvg_ref, wqt_ref, wk_ref, wvt_ref,
                  sng_ref, snb_ref, sw_ref, sbias_ref, bngc_ref,
                  ya_ref, qt_ref, k_ref, vt_ref, gbt_ref, yc_ref,
                  xbuf, hcar, abuf, bbuf):
    @pl.when(pl.program_id(1) == 0)
    def _():
        xbuf[0:SUBLANES, :] = jnp.zeros((SUBLANES, LRU_WIDTH), F32)
        hcar[...] = jnp.zeros_like(hcar)

    x = x_ref[0]
    ms = jnp.mean(x * x, axis=-1, keepdims=True)
    h = (x * lax.rsqrt(ms + EPS) * preg_ref[...]).astype(BF16)

    def proj(lo, hi):
        return jnp.dot(h, wtok_ref[:, lo:hi], preferred_element_type=F32)

    gbt = lax.dot_general(wgbt_ref[...], h, NT_DIMS, preferred_element_type=F32)
    gbt_ref[0] = gbt.astype(gbt_ref.dtype)

    _mla_prep(proj(COL_LAT, COL_C), ct_ref[0], st_ref[0],
              c128_ref[0], s128_ref[0], qg_ref, kvg_ref, wqt_ref, wk_ref, wvt_ref,
              qt_ref, k_ref, vt_ref)

    yc = _sgu_branch(proj(COL_C, N_TOK), sng_ref, snb_ref, sw_ref, sbias_ref, bngc_ref)
    yc_ref[0] = yc.astype(yc_ref.dtype)

    ya = _rglru_branch(proj(COL_A, COL_LAT), convw_ref, convb_ref, wg_ref, bg_ref, lam_ref,
                       bnga_ref, xbuf, hcar, abuf, bbuf)
    ya_ref[0] = ya.astype(ya_ref.dtype)


def _front(x, ct, st, c128, s128, p, tm):
    B, S, D = x.shape
    weights = [p[n] for n in (
        'pre_g', 'w_tok', 'w_gbt',
        'conv_w', 'conv_b', 'w_gate', 'b_gate', 'lam', 'bng_a',
        'q_g', 'kv_g', 'wqt', 'wk', 'wvt',
        'sgu_g', 'sgu_nb', 'w_stack', 'sgu_bias', 'bng_c')]
    full = lambda a: pl.BlockSpec(a.shape, lambda b, i: (0,) * a.ndim)
    return pl.pallas_call(
        _front_kernel,
        grid=(B, S // tm),
        in_specs=[
            pl.BlockSpec((1, tm, D), lambda b, i: (b, i, 0)),
            pl.BlockSpec((1, MLA_ROPE, tm), lambda b, i: (b, 0, i)),
            pl.BlockSpec((1, MLA_ROPE, tm), lambda b, i: (b, 0, i)),
            pl.BlockSpec((1, tm, LANES), lambda b, i: (b, i, 0)),
            pl.BlockSpec((1, tm, LANES), lambda b, i: (b, i, 0)),
        ] + [full(w) for w in weights],
        out_specs=[
            pl.BlockSpec((1, tm, LRU_WIDTH), lambda b, i: (b, i, 0)),
            pl.BlockSpec((1, MLA_HEADS, QK_PAD, tm), lambda b, i: (b, 0, 0, i)),
            pl.BlockSpec((1, MLA_HEADS, tm, QK_PAD), lambda b, i: (b, 0, i, 0)),
            pl.BlockSpec((1, MLA_HEADS, MLA_V, tm), lambda b, i: (b, 0, 0, i)),
            pl.BlockSpec((1, MLA_WIDTH, tm), lambda b, i: (b, 0, i)),
            pl.BlockSpec((1, tm, SGU_WIDTH), lambda b, i: (b, i, 0)),
        ],
        out_shape=[
            jax.ShapeDtypeStruct((B, S, LRU_WIDTH), BF16),
            jax.ShapeDtypeStruct((B, MLA_HEADS, QK_PAD, S), BF16),
            jax.ShapeDtypeStruct((B, MLA_HEADS, S, QK_PAD), BF16),
            jax.ShapeDtypeStruct((B, MLA_HEADS, MLA_V, S), BF16),
            jax.ShapeDtypeStruct((B, MLA_WIDTH, S), BF16),
            jax.ShapeDtypeStruct((B, S, SGU_WIDTH), BF16),
        ],
        scratch_shapes=[
            pltpu.VMEM((tm + SUBLANES, LRU_WIDTH), F32),
            pltpu.VMEM((1, LRU_WIDTH), F32),
            pltpu.VMEM((tm, LRU_WIDTH), F32),
            pltpu.VMEM((tm, LRU_WIDTH), F32),
        ],
        compiler_params=_cparams("parallel", "arbitrary"),
        name="front",
    )(x, ct, st, c128, s128, *weights)


def _attn_kernel(qt_ref, k_ref, vt_ref, o_ref, s_even, s_odd, m_scr):
    i = pl.program_id(2)
    nq = pl.num_programs(2) - 1
    tq = qt_ref.shape[3]
    tk = tq
    wslot = i % 2
    rslot = 1 - wslot

    @pl.when(i == 0)
    def _():
        m_scr[...] = jnp.zeros_like(m_scr)

    qt = qt_ref[0, 0]
    m_prev = m_scr[rslot]

    def fold(x):
        return x.reshape(x.shape[0] // SUBLANES, SUBLANES, tq)

    def scores(ks):
        k = k_ref[0, 0, pl.ds(ks, tk), :]
        return jnp.dot(k, qt, preferred_element_type=F32)

    def run(s_w, s_r):
        def score_tile(ks, mrun):
            s = scores(ks)
            s_w[pl.ds(ks, tk), :] = s
            return jnp.maximum(mrun, jnp.max(fold(s), axis=0))

        def finish_tile(ks, lrun, acc):
            p = jnp.exp2(s_r[pl.ds(ks, tk), :] - m_prev)
            lrun = lrun + jnp.sum(fold(p), axis=0)
            v = vt_ref[0, 0, :, pl.ds(ks, tk)]
            acc = acc + jnp.dot(v, p.astype(BF16), preferred_element_type=F32)
            return lrun, acc

        def both(nsub):
            def body(j, carry):
                mrun, lrun, acc = carry
                for u in range(nsub):
                    ks = pl.multiple_of((j * nsub + u) * tk, tk)
                    mrun = score_tile(ks, mrun)
                    lrun, acc = finish_tile(ks, lrun, acc)
                return mrun, lrun, acc
            return body

        def finish_only(nsub):
            def body(j, carry):
                lrun, acc = carry
                for u in range(nsub):
                    ks = pl.multiple_of((j * nsub + u) * tk, tk)
                    lrun, acc = finish_tile(ks, lrun, acc)
                return lrun, acc
            return body

        n_both = jnp.where(i < nq, i, 0)
        n_big = n_both // ATT_KBIG
        carry = (jnp.full((SUBLANES, tq), -jnp.inf, F32), jnp.zeros((SUBLANES, tq), F32),
                 jnp.zeros((MLA_V, tq), F32))
        carry = lax.fori_loop(0, n_big, both(ATT_KBIG), carry)
        mrun, lrun, acc = lax.fori_loop(n_big * ATT_KBIG, n_both, both(1), carry)
        n_only = i - n_both
        n_big = n_only // ATT_KBIG
        carry = lax.fori_loop(0, n_big, finish_only(ATT_KBIG), (lrun, acc))
        lrun, acc = lax.fori_loop(n_big * ATT_KBIG, n_only, finish_only(1), carry)

        @pl.when(i < nq)
        def _():
            ks = pl.multiple_of(i * tk, tk)
            s = scores(ks)
            kc = lax.broadcasted_iota(jnp.int32, s.shape, 0) // CHUNK
            qc = lax.broadcasted_iota(jnp.int32, s.shape, 1) // CHUNK
            s = jnp.where(kc <= qc, s, -jnp.inf)
            s_w[pl.ds(ks, tk), :] = s
            mtile = jnp.maximum(mrun, jnp.max(fold(s), axis=0))
            m_scr[wslot] = jnp.max(mtile, axis=0, keepdims=True)

        return lrun, acc

    lrun, acc = lax.cond(wslot == 0, lambda: run(s_even, s_odd), lambda: run(s_odd, s_even))

    @pl.when(i > 0)
    def _():
        l = jnp.sum(lrun, axis=0, keepdims=True)
        o_ref[0, 0] = (acc / l).astype(o_ref.dtype)


def _attention(qt, k, vt, tq):
    B, H, _, S = qt.shape
    nq = S // tq
    return pl.pallas_call(
        _attn_kernel,
        grid=(B, H, nq + 1),
        in_specs=[
            pl.BlockSpec((1, 1, QK_PAD, tq), lambda b, h, i: (b, h, 0, jnp.minimum(i, nq - 1))),
            pl.BlockSpec((1, 1, S, QK_PAD), lambda b, h, i: (b, h, 0, 0)),
            pl.BlockSpec((1, 1, MLA_V, S), lambda b, h, i: (b, h, 0, 0)),
        ],
        out_specs=pl.BlockSpec((1, 1, MLA_V, tq), lambda b, h, i: (b, h, 0, jnp.maximum(i - 1, 0))),
        out_shape=jax.ShapeDtypeStruct((B, H, MLA_V, S), BF16),
        scratch_shapes=[pltpu.VMEM((S, tq), F32), pltpu.VMEM((S, tq), F32),
                        pltpu.VMEM((2, 1, tq), F32)],
        compiler_params=pltpu.CompilerParams(
            dimension_semantics=("parallel", "parallel", "arbitrary"),
            vmem_limit_bytes=ATT_VMEM_LIMIT),
        name="attention",
    )(qt, k, vt)


def _outproj_kernel(x_ref, ya_ref, ot_ref, gbt_ref, yc_ref, bngb_ref, w1_ref, w2_ref, w3_ref,
                    pg_ref, out_ref):
    gbt = gbt_ref[0].astype(F32)
    ybt = ot_ref[0].astype(F32) * _silu(gbt)
    ms = jnp.mean(ybt * ybt, axis=0, keepdims=True)
    ybt = (ybt * lax.rsqrt(ms + EPS) * bngb_ref[...]).astype(BF16)
    y = jnp.dot(ya_ref[0], w1_ref[...], preferred_element_type=F32)
    y = y + lax.dot_general(ybt, w2_ref[...], TN_DIMS, preferred_element_type=F32)
    y = y + jnp.dot(yc_ref[0], w3_ref[...], preferred_element_type=F32)
    ms = jnp.mean(y * y, axis=-1, keepdims=True)
    out_ref[0] = x_ref[0] + y * lax.rsqrt(ms + EPS) * pg_ref[...]


def _outproj(x, ya, ot, gbt, yc, bng_b, w1, w2, w3, post_g, tm):
    B, S, D = x.shape
    full = lambda shape: pl.BlockSpec(shape, lambda b, i: (0,) * len(shape))
    return pl.pallas_call(
        _outproj_kernel,
        grid=(B, S // tm),
        in_specs=[
            pl.BlockSpec((1, tm, D), lambda b, i: (b, i, 0)),
            pl.BlockSpec((1, tm, LRU_WIDTH), lambda b, i: (b, i, 0)),
            pl.BlockSpec((1, MLA_WIDTH, tm), lambda b, i: (b, 0, i)),
            pl.BlockSpec((1, MLA_WIDTH, tm), lambda b, i: (b, 0, i)),
            pl.BlockSpec((1, tm, SGU_WIDTH), lambda b, i: (b, i, 0)),
            full((MLA_WIDTH, 1)),
            full((LRU_WIDTH, D)),
            full((MLA_WIDTH, D)),
            full((SGU_WIDTH, D)),
            full((1, D)),
        ],
        out_specs=pl.BlockSpec((1, tm, D), lambda b, i: (b, i, 0)),
        out_shape=jax.ShapeDtypeStruct((B, S, D), F32),
        compiler_params=_cparams("parallel", "parallel"),
        name="outproj",
    )(x, ya, ot, gbt, yc, bng_b, w1, w2, w3, post_g)


def _layer_params(l, pre_norm_g, w_in, conv_w, conv_b, lru_wa, lru_ba, lru_wx, lru_bx, lru_lambda,
                  q_norm_g, w_uq, kv_norm_g, w_ukv, sgu_norm_g, sgu_norm_b, sgu_w, sgu_b,
                  branch_norm_g, w_out, post_norm_g):
    D = w_in.shape[1]
    offs = np.cumsum([0, LRU_WIDTH, LRU_WIDTH, Q_RANK, KV_RANK, MLA_ROPE, MLA_WIDTH,
                      SGU_WIDTH, SGU_WIDTH, SGU_WIDTH])
    win = w_in[l]
    cols = lambda i: win[:, offs[i]:offs[i + 1]]
    zeros = lambda n: jnp.zeros((D, n), win.dtype)
    w_tok = jnp.concatenate([
        cols(0), cols(1),
        cols(2), cols(4), zeros(Q_LAT_PAD - Q_RANK - MLA_ROPE), cols(3),
        cols(6), cols(7), cols(8),
    ], axis=1).astype(BF16)
    w_gbt = cols(5).T.astype(BF16)

    def block_diag(w):
        eye = jnp.eye(LRU_HEADS, dtype=w.dtype)
        return jnp.einsum('hij,hg->higj', w, eye).reshape(LRU_WIDTH, LRU_WIDTH)

    w_gate = jnp.concatenate([block_diag(lru_wa[l]), block_diag(lru_wx[l])], axis=1).astype(BF16)
    b_gate = jnp.concatenate([lru_ba[l], lru_bx[l]]).reshape(1, -1)

    q_g = jnp.concatenate([q_norm_g[l], jnp.zeros((Q_LAT_PAD - Q_RANK,), F32)]).reshape(1, -1)
    wqt = jnp.concatenate([w_uq[l], jnp.zeros((Q_LAT_PAD - Q_RANK, w_uq.shape[2]), F32)], axis=0)
    wqt = wqt.T.astype(BF16)
    wkv = w_ukv[l].reshape(KV_RANK, MLA_HEADS, MLA_NOPE + MLA_V)
    wk = jnp.concatenate([wkv[:, :, :MLA_NOPE],
                          jnp.zeros((KV_RANK, MLA_HEADS, LANES - MLA_NOPE), F32)], axis=2)
    wk = wk.reshape(KV_RANK, MLA_HEADS * LANES).astype(BF16)
    wvt = wkv[:, :, MLA_NOPE:].reshape(KV_RANK, MLA_HEADS * MLA_V).T.astype(BF16)

    w_stack = sgu_w[l].reshape(SGU_GROUPS * SGU_BLOCK, SGU_BLOCK)
    sgu_bias = jnp.repeat(sgu_b[l].T, SGU_GROUP_DIM, axis=1)

    bng = branch_norm_g[l]
    wo = w_out[l].astype(BF16)
    return dict(
        pre_g=pre_norm_g[l].reshape(1, -1), w_tok=w_tok, w_gbt=w_gbt,
        conv_w=conv_w[l], conv_b=conv_b[l].reshape(1, -1), w_gate=w_gate, b_gate=b_gate,
        lam=lru_lambda[l].reshape(1, -1),
        q_g=q_g, kv_g=kv_norm_g[l].reshape(1, -1), wqt=wqt, wk=wk, wvt=wvt,
        sgu_g=sgu_norm_g[l].reshape(1, -1), sgu_nb=sgu_norm_b[l].reshape(1, -1),
        w_stack=w_stack, sgu_bias=sgu_bias,
        bng_a=bng[:LRU_WIDTH].reshape(1, -1),
        bng_b=bng[LRU_WIDTH:LRU_WIDTH + MLA_WIDTH].reshape(-1, 1),
        bng_c=bng[LRU_WIDTH + MLA_WIDTH:].reshape(1, -1),
        w1=wo[:LRU_WIDTH], w2=wo[LRU_WIDTH:LRU_WIDTH + MLA_WIDTH], w3=wo[LRU_WIDTH + MLA_WIDTH:],
        post_g=post_norm_g[l].reshape(1, -1),
    )


def kernel(x, positions, pre_norm_g, w_in, conv_w, conv_b, lru_wa, lru_ba, lru_wx, lru_bx, lru_lambda,
           q_norm_g, w_uq, kv_norm_g, w_ukv, sgu_norm_g, sgu_norm_b, sgu_w, sgu_b, branch_norm_g,
           w_out, post_norm_g):
    B, S, D = x.shape
    depth = w_in.shape[0]
    tm = min(ROW_TILE, S)
    tq = min(ATT_TQ, S)
    assert S % tm == 0 and S % tq == 0 and tq % CHUNK == 0 and tm % SGU_BLOCK == 0

    ct, st, c128, s128 = _rope_tables(positions, tm)
    h = x
    for l in range(depth):
        p = _layer_params(l, pre_norm_g, w_in, conv_w, conv_b, lru_wa, lru_ba, lru_wx, lru_bx,
                          lru_lambda, q_norm_g, w_uq, kv_norm_g, w_ukv, sgu_norm_g, sgu_norm_b,
                          sgu_w, sgu_b, branch_norm_g, w_out, post_norm_g)
        ya, qt, k, vt, gbt, yc = _front(h, ct, st, c128, s128, p, tm)
        ot = _attention(qt, k, vt, tq).reshape(B, MLA_WIDTH, S)
        h = _outproj(h, ya, ot, gbt, yc, p['bng_b'], p['w1'], p['w2'], p['w3'], p['post_g'], tm)
    return h
```

```python
import math

import numpy as np
import jax
import jax.numpy as jnp
from jax import lax
from jax.experimental import pallas as pl
from jax.experimental.pallas import tpu as pltpu

F32 = jnp.float32
BF16 = jnp.bfloat16

EPS = 1e-6
CHUNK = 64

LRU_WIDTH = 384
LRU_HEADS = 6
LRU_HEAD_DIM = 64
CONV_WIDTH = 4
LRU_C = 8.0

MLA_HEADS = 6
MLA_NOPE = 64
MLA_ROPE = 32
MLA_V = 64
MLA_QK = MLA_NOPE + MLA_ROPE
MLA_WIDTH = MLA_HEADS * MLA_V
Q_RANK = 192
KV_RANK = 128
ROPE_THETA = 10000.0

SGU_WIDTH = 256
SGU_GROUPS = 4
SGU_GROUP_DIM = 64
SGU_BLOCK = 128

LANES = 128
SUBLANES = 8
QK_PAD = 128
Q_LAT_PAD = 256

COL_A = 0
COL_LAT = COL_A + 2 * LRU_WIDTH
COL_C = COL_LAT + Q_LAT_PAD + KV_RANK
N_TOK = COL_C + 3 * SGU_WIDTH
assert (Q_RANK - LANES) == MLA_NOPE and Q_RANK + MLA_ROPE <= Q_LAT_PAD

ROW_TILE = 512
ATT_TQ = 512
ATT_KBIG = 4
VMEM_LIMIT = 48 * 1024 * 1024
ATT_VMEM_LIMIT = 56 * 1024 * 1024

NT_DIMS = (((1,), (1,)), ((), ()))
TN_DIMS = (((0,), (0,)), ((), ()))


def _cparams(*sem):
    return pltpu.CompilerParams(dimension_semantics=sem, vmem_limit_bytes=VMEM_LIMIT)


def _silu(x):
    return x * jax.nn.sigmoid(x)


def _gelu(x):
    c = math.sqrt(2.0 / math.pi)
    return 0.5 * x * (1.0 + jnp.tanh(c * (x + 0.044715 * (x * x * x))))


def _rope_table_kernel(pos_ref, inv_ref, ct_ref, st_ref, c128_ref, s128_ref):
    pos = pos_ref[0].astype(F32)
    ang = inv_ref[...] * pos
    cos = jnp.cos(ang)
    sin = jnp.sin(ang)
    row = lax.broadcasted_iota(jnp.int32, ang.shape, 0)
    sin_signed = jnp.where(row < MLA_ROPE // 2, -sin, sin)
    ct_ref[0] = cos
    st_ref[0] = sin_signed
    ts = ang.shape[1]
    zeros_lo = jnp.zeros((MLA_NOPE, ts), F32)
    zeros_hi = jnp.zeros((LANES - MLA_QK, ts), F32)
    c128_ref[0] = jnp.concatenate([zeros_lo, cos, zeros_hi], axis=0).T
    s128_ref[0] = jnp.concatenate([zeros_lo, sin_signed, zeros_hi], axis=0).T


def _rope_tables(positions, ts):
    B, S = positions.shape
    half = MLA_ROPE // 2
    inv_freq = ROPE_THETA ** (-jnp.arange(half, dtype=F32) / half)
    inv2 = jnp.concatenate([inv_freq, inv_freq]).reshape(MLA_ROPE, 1)
    pos3 = positions.reshape(B, 1, S)
    return pl.pallas_call(
        _rope_table_kernel,
        grid=(B, S // ts),
        in_specs=[
            pl.BlockSpec((1, 1, ts), lambda b, i: (b, 0, i)),
            pl.BlockSpec((MLA_ROPE, 1), lambda b, i: (0, 0)),
        ],
        out_specs=[
            pl.BlockSpec((1, MLA_ROPE, ts), lambda b, i: (b, 0, i)),
            pl.BlockSpec((1, MLA_ROPE, ts), lambda b, i: (b, 0, i)),
            pl.BlockSpec((1, ts, LANES), lambda b, i: (b, i, 0)),
            pl.BlockSpec((1, ts, LANES), lambda b, i: (b, i, 0)),
        ],
        out_shape=[
            jax.ShapeDtypeStruct((B, MLA_ROPE, S), F32),
            jax.ShapeDtypeStruct((B, MLA_ROPE, S), F32),
            jax.ShapeDtypeStruct((B, S, LANES), F32),
            jax.ShapeDtypeStruct((B, S, LANES), F32),
        ],
        compiler_params=_cparams("parallel", "parallel"),
        name="rope_tables",
    )(pos3, inv2)


def _scan8(a, b):
    row = lax.broadcasted_iota(jnp.int32, a.shape, 0)
    for d in (1, 2, 4):
        a_prev = pltpu.roll(a, d, 0)
        b_prev = pltpu.roll(b, d, 0)
        live = row >= d
        b = jnp.where(live, a * b_prev + b, b)
        a = jnp.where(live, a * a_prev, a)
    return a, b


def _rglru_inputs(a_in, convw_ref, convb_ref, wg_ref, bg_ref, lam_ref, xbuf, abuf, bbuf):
    ts = a_in.shape[0]
    halo = SUBLANES
    xa = a_in[:, 0:LRU_WIDTH]
    ga = a_in[:, LRU_WIDTH:2 * LRU_WIDTH]
    xbuf[halo:halo + ts, :] = xa
    xc = convb_ref[...]
    for k in range(CONV_WIDTH):
        off = halo - (CONV_WIDTH - 1) + k
        xc = xc + convw_ref[k:k + 1, :] * xbuf[off:off + ts, :]
    xbuf[0:halo, :] = xbuf[ts:ts + halo, :]

    gz = jnp.dot(xc.astype(BF16), wg_ref[...], preferred_element_type=F32) + bg_ref[...]
    gate_a = jax.nn.sigmoid(gz[:, 0:LRU_WIDTH])
    gate_x = jax.nn.sigmoid(gz[:, LRU_WIDTH:2 * LRU_WIDTH])
    nl = -lam_ref[...]
    softplus = jnp.maximum(nl, 0.0) + jnp.log(1.0 + jnp.exp(-jnp.abs(nl)))
    log_a = (-LRU_C) * gate_a * softplus
    a = jnp.exp(log_a)
    mult = jnp.sqrt(1.0 - a * a)
    abuf[...] = a
    bbuf[...] = mult * (gate_x * xc)
    return _silu(ga)


def _rglru_scan(hcar, abuf, bbuf):
    def blk(i, h):
        r = pl.multiple_of(i * SUBLANES, SUBLANES)
        a8, b8 = _scan8(abuf[pl.ds(r, SUBLANES), :], bbuf[pl.ds(r, SUBLANES), :])
        rows = a8 * h + b8
        bbuf[pl.ds(r, SUBLANES), :] = rows
        return rows[SUBLANES - 1:SUBLANES, :]

    hcar[...] = lax.fori_loop(0, abuf.shape[0] // SUBLANES, blk, hcar[...])


def _rglru_output(g_silu, bbuf, bng_ref):
    ya = bbuf[...] * g_silu
    ms = jnp.mean(ya * ya, axis=-1, keepdims=True)
    return ya * lax.rsqrt(ms + EPS) * bng_ref[...]


def _mla_prep(lat, ct, st, c128, s128, qg_ref, kvg_ref, wqt_ref, wk_ref, wvt_ref,
              qt_ref, k_ref, vt_ref):
    ts = lat.shape[0]
    q_blk = lat[:, 0:Q_LAT_PAD]
    kv_lat = lat[:, Q_LAT_PAD:Q_LAT_PAD + KV_RANK]
    q_lane = lax.broadcasted_iota(jnp.int32, q_blk.shape, 1)
    q_only = jnp.where(q_lane < Q_RANK, q_blk, 0.0)
    q_ms = jnp.sum(q_only * q_only, axis=-1, keepdims=True) * (1.0 / Q_RANK)
    qn = (q_blk * lax.rsqrt(q_ms + EPS) * qg_ref[...]).astype(BF16)
    kv_ms = jnp.mean(kv_lat * kv_lat, axis=-1, keepdims=True)
    kvn = (kv_lat * lax.rsqrt(kv_ms + EPS) * kvg_ref[...]).astype(BF16)

    qt = lax.dot_general(wqt_ref[...], qn, NT_DIMS, preferred_element_type=F32)
    half = MLA_ROPE // 2
    q_scale = (MLA_QK ** -0.5) * math.log2(math.e)
    pad = jnp.zeros((QK_PAD - MLA_QK, ts), F32)
    for h in range(MLA_HEADS):
        base = h * MLA_QK
        nope = qt[base:base + MLA_NOPE, :]
        pe = qt[base + MLA_NOPE:base + MLA_QK, :]
        pe_sw = jnp.concatenate([pe[half:, :], pe[:half, :]], axis=0)
        pe = pe * ct + pe_sw * st
        qh = jnp.concatenate([nope, pe, pad], axis=0) * q_scale
        qt_ref[0, h] = qh.astype(qt_ref.dtype)

    kn = jnp.dot(kvn, wk_ref[...], preferred_element_type=F32)
    kr = lat[:, LANES:2 * LANES]
    half_swapped = jnp.where(lax.broadcasted_iota(jnp.int32, kr.shape, 1) < MLA_NOPE + half,
                             pltpu.roll(kr, LANES - half, 1), pltpu.roll(kr, half, 1))
    pe128 = kr * c128 + half_swapped * s128
    for h in range(MLA_HEADS):
        k_ref[0, h] = (kn[:, h * LANES:(h + 1) * LANES] + pe128).astype(k_ref.dtype)

    vt = lax.dot_general(wvt_ref[...], kvn, NT_DIMS, preferred_element_type=F32)
    for h in range(MLA_HEADS):
        vt_ref[0, h] = vt[h * MLA_V:(h + 1) * MLA_V, :].astype(vt_ref.dtype)


def _sgu_branch(c_in, ng_ref, nb_ref, w_ref, bias_ref, bng_ref):
    tm = c_in.shape[0]
    u = _gelu(c_in[:, 0:SGU_WIDTH])
    v = _gelu(c_in[:, SGU_WIDTH:2 * SGU_WIDTH])
    gc = c_in[:, 2 * SGU_WIDTH:3 * SGU_WIDTH]
    mu = jnp.mean(v, axis=-1, keepdims=True)
    vc = v - mu
    var = jnp.mean(vc * vc, axis=-1, keepdims=True)
    vn = (vc * lax.rsqrt(var + EPS) * ng_ref[...] + nb_ref[...]).astype(BF16)

    w = w_ref[...]
    wi = (lax.broadcasted_iota(jnp.int32, w.shape, 0) % SGU_BLOCK) // CHUNK
    wj = lax.broadcasted_iota(jnp.int32, w.shape, 1) // CHUNK
    w = jnp.where(wi >= wj, w, 0.0).astype(BF16)

    lane_group = lax.broadcasted_iota(jnp.int32, (SGU_BLOCK, SGU_WIDTH), 1) // SGU_GROUP_DIM
    mixed = []
    for n in range(tm // SGU_BLOCK):
        r = jnp.dot(w, vn[n * SGU_BLOCK:(n + 1) * SGU_BLOCK, :], preferred_element_type=F32)
        m = r[0:SGU_BLOCK, :]
        for g in range(1, SGU_GROUPS):
            m = jnp.where(lane_group == g, r[g * SGU_BLOCK:(g + 1) * SGU_BLOCK, :], m)
        mixed.append(m + bias_ref[...])
    mixed = jnp.concatenate(mixed, axis=0)

    yc = u * mixed * _silu(gc)
    ms = jnp.mean(yc * yc, axis=-1, keepdims=True)
    return yc * lax.rsqrt(ms + EPS) * bng_ref[...]


def _front_kernel(x_ref, ct_ref, st_ref, c128_ref, s128_ref,
                  preg_ref, wtok_ref, wgbt_ref,
                  convw_ref, convb_ref, wg_ref, bg_ref, lam_ref, bnga_ref,
                  qg_ref, kvg_ref, wqt_ref, wk_ref, wvt_ref,
                  sng_ref, snb_ref, sw_ref, sbias_ref, bngc_ref,
                  ya_ref, qt_ref, k_ref, vt_ref, gbt_ref, yc_ref,
                  xbuf, hcar, abuf, bbuf):
    @pl.when(pl.program_id(1) == 0)
    def _():
        xbuf[0:SUBLANES, :] = jnp.zeros((SUBLANES, LRU_WIDTH), F32)
        hcar[...] = jnp.zeros_like(hcar)

    x = x_ref[0]
    ms = jnp.mean(x * x, axis=-1, keepdims=True)
    h = (x * lax.rsqrt(ms + EPS) * preg_ref[...]).astype(BF16)

    def proj(lo, hi):
        return jnp.dot(h, wtok_ref[:, lo:hi], preferred_element_type=F32)

    g_silu = _rglru_inputs(proj(COL_A, COL_LAT), convw_ref, convb_ref, wg_ref, bg_ref, lam_ref,
                           xbuf, abuf, bbuf)
    lat = proj(COL_LAT, COL_C)
    yc = _sgu_branch(proj(COL_C, N_TOK), sng_ref, snb_ref, sw_ref, sbias_ref, bngc_ref)
    yc_ref[0] = yc.astype(yc_ref.dtype)
    gbt = lax.dot_general(wgbt_ref[...], h, NT_DIMS, preferred_element_type=F32)
    gbt_ref[0] = gbt.astype(gbt_ref.dtype)

    _rglru_scan(hcar, abuf, bbuf)

    ya_ref[0] = _rglru_output(g_silu, bbuf, bnga_ref).astype(ya_ref.dtype)
    _mla_prep(lat, ct_ref[0], st_ref[0], c128_ref[0], s128_ref[0], qg_ref, kvg_ref,
              wqt_ref, wk_ref, wvt_ref, qt_ref, k_ref, vt_ref)


def _front(x, ct, st, c128, s128, p, tm):
    B, S, D = x.shape
    weights = [p[n] for n in (
        'pre_g', 'w_tok', 'w_gbt',
        'conv_w', 'conv_b', 'w_gate', 'b_gate', 'lam', 'bng_a',
        'q_g', 'kv_g', 'wqt', 'wk', 'wvt',
        'sgu_g', 'sgu_nb', 'w_stack', 'sgu_bias', 'bng_c')]
    full = lambda a: pl.BlockSpec(a.shape, lambda b, i: (0,) * a.ndim)
    return pl.pallas_call(
        _front_kernel,
        grid=(B, S // tm),
        in_specs=[
            pl.BlockSpec((1, tm, D), lambda b, i: (b, i, 0)),
            pl.BlockSpec((1, MLA_ROPE, tm), lambda b, i: (b, 0, i)),
            pl.BlockSpec((1, MLA_ROPE, tm), lambda b, i: (b, 0, i)),
            pl.BlockSpec((1, tm, LANES), lambda b, i: (b, i, 0)),
            pl.BlockSpec((1, tm, LANES), lambda b, i: (b, i, 0)),
        ] + [full(w) for w in weights],
        out_specs=[
            pl.BlockSpec((1, tm, LRU_WIDTH), lambda b, i: (b, i, 0)),
            pl.BlockSpec((1, MLA_HEADS, QK_PAD, tm), lambda b, i: (b, 0, 0, i)),
            pl.BlockSpec((1, MLA_HEADS, tm, QK_PAD), lambda b, i: (b, 0, i, 0)),
            pl.BlockSpec((1, MLA_HEADS, MLA_V, tm), lambda b, i: (b, 0, 0, i)),
            pl.BlockSpec((1, MLA_WIDTH, tm), lambda b, i: (b, 0, i)),
            pl.BlockSpec((1, tm, SGU_WIDTH), lambda b, i: (b, i, 0)),
        ],
        out_shape=[
            jax.ShapeDtypeStruct((B, S, LRU_WIDTH), BF16),
            jax.ShapeDtypeStruct((B, MLA_HEADS, QK_PAD, S), BF16),
            jax.ShapeDtypeStruct((B, MLA_HEADS, S, QK_PAD), BF16),
            jax.ShapeDtypeStruct((B, MLA_HEADS, MLA_V, S), BF16),
            jax.ShapeDtypeStruct((B, MLA_WIDTH, S), BF16),
            jax.ShapeDtypeStruct((B, S, SGU_WIDTH), BF16),
        ],
        scratch_shapes=[
            pltpu.VMEM((tm + SUBLANES, LRU_WIDTH), F32),
            pltpu.VMEM((1, LRU_WIDTH), F32),
            pltpu.VMEM((tm, LRU_WIDTH), F32),
            pltpu.VMEM((tm, LRU_WIDTH), F32),
        ],
        compiler_params=_cparams("parallel", "arbitrary"),
        name="front",
    )(x, ct, st, c128, s128, *weights)


def _attn_kernel(qt_ref, k_ref, vt_ref, o_ref, s_even, s_odd, m_scr):
    i = pl.program_id(2)
    nq = pl.num_programs(2) - 1
    tq = qt_ref.shape[3]
    tk = tq
    wslot = i % 2
    rslot = 1 - wslot

    @pl.when(i == 0)
    def _():
        m_scr[...] = jnp.zeros_like(m_scr)

    qt = qt_ref[0, 0]
    m_prev = m_scr[rslot]

    def fold(x):
        return x.reshape(x.shape[0] // SUBLANES, SUBLANES, tq)

    def scores(ks):
        k = k_ref[0, 0, pl.ds(ks, tk), :]
        return jnp.dot(k, qt, preferred_element_type=F32)

    def run(s_w, s_r):
        def score_tile(ks, mrun):
            s = scores(ks)
            s_w[pl.ds(ks, tk), :] = s
            return jnp.maximum(mrun, jnp.max(fold(s), axis=0))

        def finish_tile(ks, lrun, acc):
            p = jnp.exp2(s_r[pl.ds(ks, tk), :] - m_prev)
            lrun = lrun + jnp.sum(fold(p), axis=0)
            v = vt_ref[0, 0, :, pl.ds(ks, tk)]
            acc = acc + jnp.dot(v, p.astype(BF16), preferred_element_type=F32)
            return lrun, acc

        def both(nsub):
            def body(j, carry):
                mrun, lrun, acc = carry
                for u in range(nsub):
                    ks = pl.multiple_of((j * nsub + u) * tk, tk)
                    mrun = score_tile(ks, mrun)
                    lrun, acc = finish_tile(ks, lrun, acc)
                return mrun, lrun, acc
            return body

        def finish_only(nsub):
            def body(j, carry):
                lrun, acc = carry
                for u in range(nsub):
                    ks = pl.multiple_of((j * nsub + u) * tk, tk)
                    lrun, acc = finish_tile(ks, lrun, acc)
                return lrun, acc
            return body

        def score_diag(mrun):
            ks = pl.multiple_of(i * tk, tk)
            s = scores(ks)
            kc = lax.broadcasted_iota(jnp.int32, s.shape, 0) // CHUNK
            qc = lax.broadcasted_iota(jnp.int32, s.shape, 1) // CHUNK
            s = jnp.where(kc <= qc, s, -jnp.inf)
            s_w[pl.ds(ks, tk), :] = s
            return jnp.maximum(mrun, jnp.max(fold(s), axis=0))

        scoring = i < nq
        paired = jnp.logical_and(scoring, i > 0)
        n_big = jnp.where(paired, (i - 1) // ATT_KBIG, 0)
        n_rest = (i - 1) % ATT_KBIG + 1

        def tail(ntile):
            def body(_, carry):
                mrun, lrun, acc = carry
                for u in range(ntile):
                    ks = pl.multiple_of((n_big * ATT_KBIG + u) * tk, tk)
                    mrun = score_tile(ks, mrun)
                    if u == 0:
                        mrun = score_diag(mrun)
                    lrun, acc = finish_tile(ks, lrun, acc)
                return mrun, lrun, acc
            return body

        carry = (jnp.full((SUBLANES, tq), -jnp.inf, F32), jnp.zeros((SUBLANES, tq), F32),
                 jnp.zeros((MLA_V, tq), F32))
        carry = lax.fori_loop(0, n_big, both(ATT_KBIG), carry)
        for ntile in range(1, ATT_KBIG + 1):
            run_it = jnp.logical_and(paired, n_rest == ntile)
            carry = lax.fori_loop(0, run_it.astype(jnp.int32), tail(ntile), carry)
        mrun, lrun, acc = carry
        n_only = jnp.where(scoring, 0, i)
        n_big = n_only // ATT_KBIG
        carry = lax.fori_loop(0, n_big, finish_only(ATT_KBIG), (lrun, acc))
        lrun, acc = lax.fori_loop(n_big * ATT_KBIG, n_only, finish_only(1), carry)

        @pl.when(i == 0)
        def _():
            m_scr[wslot] = jnp.max(score_diag(mrun), axis=0, keepdims=True)

        @pl.when(paired)
        def _():
            m_scr[wslot] = jnp.max(mrun, axis=0, keepdims=True)

        return lrun, acc

    lrun, acc = lax.cond(wslot == 0, lambda: run(s_even, s_odd), lambda: run(s_odd, s_even))

    @pl.when(i > 0)
    def _():
        l = jnp.sum(lrun, axis=0, keepdims=True)
        o_ref[0, 0] = (acc / l).astype(o_ref.dtype)


def _attention(qt, k, vt, tq):
    B, H, _, S = qt.shape
    nq = S // tq
    return pl.pallas_call(
        _attn_kernel,
        grid=(B, H, nq + 1),
        in_specs=[
            pl.BlockSpec((1, 1, QK_PAD, tq), lambda b, h, i: (b, h, 0, jnp.minimum(i, nq - 1))),
            pl.BlockSpec((1, 1, S, QK_PAD), lambda b, h, i: (b, h, 0, 0)),
            pl.BlockSpec((1, 1, MLA_V, S), lambda b, h, i: (b, h, 0, 0)),
        ],
        out_specs=pl.BlockSpec((1, 1, MLA_V, tq), lambda b, h, i: (b, h, 0, jnp.maximum(i - 1, 0))),
        out_shape=jax.ShapeDtypeStruct((B, H, MLA_V, S), BF16),
        scratch_shapes=[pltpu.VMEM((S, tq), F32), pltpu.VMEM((S, tq), F32),
                        pltpu.VMEM((2, 1, tq), F32)],
        compiler_params=pltpu.CompilerParams(
            dimension_semantics=("parallel", "parallel", "arbitrary"),
            vmem_limit_bytes=ATT_VMEM_LIMIT),
        name="attention",
    )(qt, k, vt)


def _outproj_kernel(x_ref, ya_ref, ot_ref, gbt_ref, yc_ref, bngb_ref, w_ref, pg_ref, out_ref):
    gbt = gbt_ref[0].astype(F32)
    ybt = ot_ref[0].astype(F32) * _silu(gbt)
    ms = jnp.mean(ybt * ybt, axis=0, keepdims=True)
    yb = (ybt * lax.rsqrt(ms + EPS) * bngb_ref[...]).T.astype(BF16)
    y = jnp.dot(jnp.concatenate([ya_ref[0], yb, yc_ref[0]], axis=1), w_ref[...],
                preferred_element_type=F32)
    ms = jnp.mean(y * y, axis=-1, keepdims=True)
    out_ref[0] = x_ref[0] + y * lax.rsqrt(ms + EPS) * pg_ref[...]


def _outproj(x, ya, ot, gbt, yc, bng_b, w_out, post_g, tm):
    B, S, D = x.shape
    full = lambda shape: pl.BlockSpec(shape, lambda b, i: (0,) * len(shape))
    return pl.pallas_call(
        _outproj_kernel,
        grid=(B, S // tm),
        in_specs=[
            pl.BlockSpec((1, tm, D), lambda b, i: (b, i, 0)),
            pl.BlockSpec((1, tm, LRU_WIDTH), lambda b, i: (b, i, 0)),
            pl.BlockSpec((1, MLA_WIDTH, tm), lambda b, i: (b, 0, i)),
            pl.BlockSpec((1, MLA_WIDTH, tm), lambda b, i: (b, 0, i)),
            pl.BlockSpec((1, tm, SGU_WIDTH), lambda b, i: (b, i, 0)),
            full((MLA_WIDTH, 1)),
            full(w_out.shape),
            full((1, D)),
        ],
        out_specs=pl.BlockSpec((1, tm, D), lambda b, i: (b, i, 0)),
        out_shape=jax.ShapeDtypeStruct((B, S, D), F32),
        compiler_params=_cparams("parallel", "parallel"),
        name="outproj",
    )(x, ya, ot, gbt, yc, bng_b, w_out, post_g)


def _layer_params(l, pre_norm_g, w_in, conv_w, conv_b, lru_wa, lru_ba, lru_wx, lru_bx, lru_lambda,
                  q_norm_g, w_uq, kv_norm_g, w_ukv, sgu_norm_g, sgu_norm_b, sgu_w, sgu_b,
                  branch_norm_g, w_out, post_norm_g):
    D = w_in.shape[1]
    offs = np.cumsum([0, LRU_WIDTH, LRU_WIDTH, Q_RANK, KV_RANK, MLA_ROPE, MLA_WIDTH,
                      SGU_WIDTH, SGU_WIDTH, SGU_WIDTH])
    win = w_in[l]
    cols = lambda i: win[:, offs[i]:offs[i + 1]]
    zeros = lambda n: jnp.zeros((D, n), win.dtype)
    w_tok = jnp.concatenate([
        cols(0), cols(1),
        cols(2), cols(4), zeros(Q_LAT_PAD - Q_RANK - MLA_ROPE), cols(3),
        cols(6), cols(7), cols(8),
    ], axis=1).astype(BF16)
    w_gbt = cols(5).T.astype(BF16)

    def block_diag(w):
        eye = jnp.eye(LRU_HEADS, dtype=w.dtype)
        return jnp.einsum('hij,hg->higj', w, eye).reshape(LRU_WIDTH, LRU_WIDTH)

    w_gate = jnp.concatenate([block_diag(lru_wa[l]), block_diag(lru_wx[l])], axis=1).astype(BF16)
    b_gate = jnp.concatenate([lru_ba[l], lru_bx[l]]).reshape(1, -1)

    q_g = jnp.concatenate([q_norm_g[l], jnp.zeros((Q_LAT_PAD - Q_RANK,), F32)]).reshape(1, -1)
    wqt = jnp.concatenate([w_uq[l], jnp.zeros((Q_LAT_PAD - Q_RANK, w_uq.shape[2]), F32)], axis=0)
    wqt = wqt.T.astype(BF16)
    wkv = w_ukv[l].reshape(KV_RANK, MLA_HEADS, MLA_NOPE + MLA_V)
    wk = jnp.concatenate([wkv[:, :, :MLA_NOPE],
                          jnp.zeros((KV_RANK, MLA_HEADS, LANES - MLA_NOPE), F32)], axis=2)
    wk = wk.reshape(KV_RANK, MLA_HEADS * LANES).astype(BF16)
    wvt = wkv[:, :, MLA_NOPE:].reshape(KV_RANK, MLA_HEADS * MLA_V).T.astype(BF16)

    w_stack = sgu_w[l].reshape(SGU_GROUPS * SGU_BLOCK, SGU_BLOCK)
    sgu_bias = jnp.repeat(sgu_b[l].T, SGU_GROUP_DIM, axis=1)

    bng = branch_norm_g[l]
    wo = w_out[l].astype(BF16)
    return dict(
        pre_g=pre_norm_g[l].reshape(1, -1), w_tok=w_tok, w_gbt=w_gbt,
        conv_w=conv_w[l], conv_b=conv_b[l].reshape(1, -1), w_gate=w_gate, b_gate=b_gate,
        lam=lru_lambda[l].reshape(1, -1),
        q_g=q_g, kv_g=kv_norm_g[l].reshape(1, -1), wqt=wqt, wk=wk, wvt=wvt,
        sgu_g=sgu_norm_g[l].reshape(1, -1), sgu_nb=sgu_norm_b[l].reshape(1, -1),
        w_stack=w_stack, sgu_bias=sgu_bias,
        bng_a=bng[:LRU_WIDTH].reshape(1, -1),
        bng_b=bng[LRU_WIDTH:LRU_WIDTH + MLA_WIDTH].reshape(-1, 1),
        bng_c=bng[LRU_WIDTH + MLA_WIDTH:].reshape(1, -1),
        w_out=wo, post_g=post_norm_g[l].reshape(1, -1),
    )


def kernel(x, positions, pre_norm_g, w_in, conv_w, conv_b, lru_wa, lru_ba, lru_wx, lru_bx, lru_lambda,
           q_norm_g, w_uq, kv_norm_g, w_ukv, sgu_norm_g, sgu_norm_b, sgu_w, sgu_b, branch_norm_g,
           w_out, post_norm_g):
    B, S, D = x.shape
    depth = w_in.shape[0]
    tm = min(ROW_TILE, S)
    tq = min(ATT_TQ, S)
    assert S % tm == 0 and S % tq == 0 and tq % CHUNK == 0 and tm % SGU_BLOCK == 0

    ct, st, c128, s128 = _rope_tables(positions, tm)
    h = x
    for l in range(depth):
        p = _layer_params(l, pre_norm_g, w_in, conv_w, conv_b, lru_wa, lru_ba, lru_wx, lru_bx,
                          lru_lambda, q_norm_g, w_uq, kv_norm_g, w_ukv, sgu_norm_g, sgu_norm_b,
                          sgu_w, sgu_b, branch_norm_g, w_out, post_norm_g)
        ya, qt, k, vt, gbt, yc = _front(h, ct, st, c128, s128, p, tm)
        ot = _attention(qt, k, vt, tq).reshape(B, MLA_WIDTH, S)
        h = _outproj(h, ya, ot, gbt, yc, p['bng_b'], p['w_out'], p['post_g'], tm)
    return h
```

```python
import math

import numpy as np
import jax
import jax.numpy as jnp
from jax import lax
from jax.experimental import pallas as pl
from jax.experimental.pallas import tpu as pltpu

F32 = jnp.float32
BF16 = jnp.bfloat16

EPS = 1e-6
CHUNK = 64

LRU_WIDTH = 384
LRU_HEADS = 6
LRU_HEAD_DIM = 64
CONV_WIDTH = 4
LRU_C = 8.0

MLA_HEADS = 6
MLA_NOPE = 64
MLA_ROPE = 32
MLA_V = 64
MLA_QK = MLA_NOPE + MLA_ROPE
MLA_WIDTH = MLA_HEADS * MLA_V
Q_RANK = 192
KV_RANK = 128
ROPE_THETA = 10000.0

SGU_WIDTH = 256
SGU_GROUPS = 4
SGU_GROUP_DIM = 64
SGU_BLOCK = 128

LANES = 128
SUBLANES = 8
QK_PAD = 128
Q_LAT_PAD = 256

COL_A = 0
COL_LAT = COL_A + 2 * LRU_WIDTH
COL_C = COL_LAT + Q_LAT_PAD + KV_RANK
N_TOK = COL_C + 3 * SGU_WIDTH
assert (Q_RANK - LANES) == MLA_NOPE and Q_RANK + MLA_ROPE <= Q_LAT_PAD

ROW_TILE = 512
OUT_TILE = 1024
ATT_TQ = 512
ATT_KBIG = 4
VMEM_LIMIT = 48 * 1024 * 1024
ATT_VMEM_LIMIT = 56 * 1024 * 1024

NT_DIMS = (((1,), (1,)), ((), ()))
TN_DIMS = (((0,), (0,)), ((), ()))


def _cparams(*sem):
    return pltpu.CompilerParams(dimension_semantics=sem, vmem_limit_bytes=VMEM_LIMIT)


def _silu(x):
    return x * jax.nn.sigmoid(x)


def _gelu(x):
    c = math.sqrt(2.0 / math.pi)
    return 0.5 * x * (1.0 + jnp.tanh(c * (x + 0.044715 * (x * x * x))))


def _rope_table_kernel(pos_ref, inv_ref, ct_ref, st_ref, c128_ref, s128_ref):
    pos = pos_ref[0].astype(F32)
    ang = inv_ref[...] * pos
    cos = jnp.cos(ang)
    sin = jnp.sin(ang)
    row = lax.broadcasted_iota(jnp.int32, ang.shape, 0)
    sin_signed = jnp.where(row < MLA_ROPE // 2, -sin, sin)
    ct_ref[0] = cos
    st_ref[0] = sin_signed
    ts = ang.shape[1]
    zeros_lo = jnp.zeros((MLA_NOPE, ts), F32)
    zeros_hi = jnp.zeros((LANES - MLA_QK, ts), F32)
    c128_ref[0] = jnp.concatenate([zeros_lo, cos, zeros_hi], axis=0).T
    s128_ref[0] = jnp.concatenate([zeros_lo, sin_signed, zeros_hi], axis=0).T


def _rope_tables(positions, ts):
    B, S = positions.shape
    half = MLA_ROPE // 2
    inv_freq = ROPE_THETA ** (-jnp.arange(half, dtype=F32) / half)
    inv2 = jnp.concatenate([inv_freq, inv_freq]).reshape(MLA_ROPE, 1)
    pos3 = positions.reshape(B, 1, S)
    return pl.pallas_call(
        _rope_table_kernel,
        grid=(B, S // ts),
        in_specs=[
            pl.BlockSpec((1, 1, ts), lambda b, i: (b, 0, i)),
            pl.BlockSpec((MLA_ROPE, 1), lambda b, i: (0, 0)),
        ],
        out_specs=[
            pl.BlockSpec((1, MLA_ROPE, ts), lambda b, i: (b, 0, i)),
            pl.BlockSpec((1, MLA_ROPE, ts), lambda b, i: (b, 0, i)),
            pl.BlockSpec((1, ts, LANES), lambda b, i: (b, i, 0)),
            pl.BlockSpec((1, ts, LANES), lambda b, i: (b, i, 0)),
        ],
        out_shape=[
            jax.ShapeDtypeStruct((B, MLA_ROPE, S), F32),
            jax.ShapeDtypeStruct((B, MLA_ROPE, S), F32),
            jax.ShapeDtypeStruct((B, S, LANES), F32),
            jax.ShapeDtypeStruct((B, S, LANES), F32),
        ],
        compiler_params=_cparams("parallel", "parallel"),
        name="rope_tables",
    )(pos3, inv2)


def _scan8(a, b):
    row = lax.broadcasted_iota(jnp.int32, a.shape, 0)
    for d in (1, 2, 4):
        a_prev = pltpu.roll(a, d, 0)
        b_prev = pltpu.roll(b, d, 0)
        live = row >= d
        b = jnp.where(live, a * b_prev + b, b)
        a = jnp.where(live, a * a_prev, a)
    return a, b


def _rglru_inputs(a_in, convw_ref, convb_ref, wg_ref, bg_ref, lam_ref, xbuf, abuf, bbuf):
    ts = a_in.shape[0]
    halo = SUBLANES
    xa = a_in[:, 0:LRU_WIDTH]
    ga = a_in[:, LRU_WIDTH:2 * LRU_WIDTH]
    xbuf[halo:halo + ts, :] = xa
    xc = convb_ref[...]
    for k in range(CONV_WIDTH):
        off = halo - (CONV_WIDTH - 1) + k
        xc = xc + convw_ref[k:k + 1, :] * xbuf[off:off + ts, :]
    xbuf[0:halo, :] = xbuf[ts:ts + halo, :]

    gz = jnp.dot(xc.astype(BF16), wg_ref[...], preferred_element_type=F32) + bg_ref[...]
    gate_a = jax.nn.sigmoid(gz[:, 0:LRU_WIDTH])
    gate_x = jax.nn.sigmoid(gz[:, LRU_WIDTH:2 * LRU_WIDTH])
    nl = -lam_ref[...]
    softplus = jnp.maximum(nl, 0.0) + jnp.log(1.0 + jnp.exp(-jnp.abs(nl)))
    log_a = (-LRU_C) * gate_a * softplus
    a = jnp.exp(log_a)
    mult = jnp.sqrt(1.0 - a * a)
    abuf[...] = a
    bbuf[...] = mult * (gate_x * xc)
    return _silu(ga)


def _rglru_scan(hcar, abuf, bbuf):
    def blk(i, h):
        r = pl.multiple_of(i * SUBLANES, SUBLANES)
        a8, b8 = _scan8(abuf[pl.ds(r, SUBLANES), :], bbuf[pl.ds(r, SUBLANES), :])
        rows = a8 * h + b8
        bbuf[pl.ds(r, SUBLANES), :] = rows
        return rows[SUBLANES - 1:SUBLANES, :]

    hcar[...] = lax.fori_loop(0, abuf.shape[0] // SUBLANES, blk, hcar[...])


def _rglru_output(g_silu, bbuf, bng_ref):
    ya = bbuf[...] * g_silu
    ms = jnp.mean(ya * ya, axis=-1, keepdims=True)
    return ya * lax.rsqrt(ms + EPS) * bng_ref[...]


def _mla_prep(lat, ct, st, c128, s128, qg_ref, kvg_ref, wqt_ref, wk_ref, wvt_ref,
              qt_ref, k_ref, vt_ref):
    ts = lat.shape[0]
    q_blk = lat[:, 0:Q_LAT_PAD]
    kv_lat = lat[:, Q_LAT_PAD:Q_LAT_PAD + KV_RANK]
    q_lane = lax.broadcasted_iota(jnp.int32, q_blk.shape, 1)
    q_only = jnp.where(q_lane < Q_RANK, q_blk, 0.0)
    q_ms = jnp.sum(q_only * q_only, axis=-1, keepdims=True) * (1.0 / Q_RANK)
    qn = (q_blk * lax.rsqrt(q_ms + EPS) * qg_ref[...]).astype(BF16)
    kv_ms = jnp.mean(kv_lat * kv_lat, axis=-1, keepdims=True)
    kvn = (kv_lat * lax.rsqrt(kv_ms + EPS) * kvg_ref[...]).astype(BF16)

    qt = lax.dot_general(wqt_ref[...], qn, NT_DIMS, preferred_element_type=F32)
    half = MLA_ROPE // 2
    q_scale = (MLA_QK ** -0.5) * math.log2(math.e)
    pad = jnp.zeros((QK_PAD - MLA_QK, ts), F32)
    for h in range(MLA_HEADS):
        base = h * MLA_QK
        nope = qt[base:base + MLA_NOPE, :]
        pe = qt[base + MLA_NOPE:base + MLA_QK, :]
        pe_sw = jnp.concatenate([pe[half:, :], pe[:half, :]], axis=0)
        pe = pe * ct + pe_sw * st
        qh = jnp.concatenate([nope, pe, pad], axis=0) * q_scale
        qt_ref[0, h] = qh.astype(qt_ref.dtype)

    kn = jnp.dot(kvn, wk_ref[...], preferred_element_type=F32)
    kr = lat[:, LANES:2 * LANES]
    half_swapped = jnp.where(lax.broadcasted_iota(jnp.int32, kr.shape, 1) < MLA_NOPE + half,
                             pltpu.roll(kr, LANES - half, 1), pltpu.roll(kr, half, 1))
    pe128 = kr * c128 + half_swapped * s128
    for h in range(MLA_HEADS):
        k_ref[0, h] = (kn[:, h * LANES:(h + 1) * LANES] + pe128).astype(k_ref.dtype)

    vt = lax.dot_general(wvt_ref[...], kvn, NT_DIMS, preferred_element_type=F32)
    for h in range(MLA_HEADS):
        vt_ref[0, h] = vt[h * MLA_V:(h + 1) * MLA_V, :].astype(vt_ref.dtype)


def _sgu_branch(c_in, ng_ref, nb_ref, w_ref, bias_ref, bng_ref):
    tm = c_in.shape[0]
    u = _gelu(c_in[:, 0:SGU_WIDTH])
    v = _gelu(c_in[:, SGU_WIDTH:2 * SGU_WIDTH])
    gc = c_in[:, 2 * SGU_WIDTH:3 * SGU_WIDTH]
    mu = jnp.mean(v, axis=-1, keepdims=True)
    vc = v - mu
    var = jnp.mean(vc * vc, axis=-1, keepdims=True)
    vn = (vc * lax.rsqrt(var + EPS) * ng_ref[...] + nb_ref[...]).astype(BF16)

    w = w_ref[...]
    wi = (lax.broadcasted_iota(jnp.int32, w.shape, 0) % SGU_BLOCK) // CHUNK
    wj = lax.broadcasted_iota(jnp.int32, w.shape, 1) // CHUNK
    w = jnp.where(wi >= wj, w, 0.0).astype(BF16)

    lane_group = lax.broadcasted_iota(jnp.int32, (SGU_BLOCK, SGU_WIDTH), 1) // SGU_GROUP_DIM
    mixed = []
    for n in range(tm // SGU_BLOCK):
        r = jnp.dot(w, vn[n * SGU_BLOCK:(n + 1) * SGU_BLOCK, :], preferred_element_type=F32)
        m = r[0:SGU_BLOCK, :]
        for g in range(1, SGU_GROUPS):
            m = jnp.where(lane_group == g, r[g * SGU_BLOCK:(g + 1) * SGU_BLOCK, :], m)
        mixed.append(m + bias_ref[...])
    mixed = jnp.concatenate(mixed, axis=0)

    yc = u * mixed * _silu(gc)
    ms = jnp.mean(yc * yc, axis=-1, keepdims=True)
    return yc * lax.rsqrt(ms + EPS) * bng_ref[...]


def _front_kernel(x_ref, ct_ref, st_ref, c128_ref, s128_ref,
                  preg_ref, wtok_ref, wgbt_ref,
                  convw_ref, convb_ref, wg_ref, bg_ref, lam_ref, bnga_ref,
                  qg_ref, kvg_ref, wqt_ref, wk_ref, wvt_ref,
                  sng_ref, snb_ref, sw_ref, sbias_ref, bngc_ref,
                  ya_ref, qt_ref, k_ref, vt_ref, gbt_ref, yc_ref,
                  xbuf, hcar, abuf, bbuf):
    @pl.when(pl.program_id(1) == 0)
    def _():
        xbuf[0:SUBLANES, :] = jnp.zeros((SUBLANES, LRU_WIDTH), F32)
        hcar[...] = jnp.zeros_like(hcar)

    x = x_ref[0]
    ms = jnp.mean(x * x, axis=-1, keepdims=True)
    h = (x * lax.rsqrt(ms + EPS) * preg_ref[...]).astype(BF16)

    def proj(lo, hi):
        return jnp.dot(h, wtok_ref[:, lo:hi], preferred_element_type=F32)

    g_silu = _rglru_inputs(proj(COL_A, COL_LAT), convw_ref, convb_ref, wg_ref, bg_ref, lam_ref,
                           xbuf, abuf, bbuf)
    lat = proj(COL_LAT, COL_C)
    yc = _sgu_branch(proj(COL_C, N_TOK), sng_ref, snb_ref, sw_ref, sbias_ref, bngc_ref)
    yc_ref[0] = yc.astype(yc_ref.dtype)
    gbt = lax.dot_general(wgbt_ref[...], h, NT_DIMS, preferred_element_type=F32)
    gbt_ref[0] = gbt.astype(gbt_ref.dtype)

    _rglru_scan(hcar, abuf, bbuf)

    ya_ref[0] = _rglru_output(g_silu, bbuf, bnga_ref).astype(ya_ref.dtype)
    _mla_prep(lat, ct_ref[0], st_ref[0], c128_ref[0], s128_ref[0], qg_ref, kvg_ref,
              wqt_ref, wk_ref, wvt_ref, qt_ref, k_ref, vt_ref)


def _front(x, ct, st, c128, s128, p, tm):
    B, S, D = x.shape
    weights = [p[n] for n in (
        'pre_g', 'w_tok', 'w_gbt',
        'conv_w', 'conv_b', 'w_gate', 'b_gate', 'lam', 'bng_a',
        'q_g', 'kv_g', 'wqt', 'wk', 'wvt',
        'sgu_g', 'sgu_nb', 'w_stack', 'sgu_bias', 'bng_c')]
    full = lambda a: pl.BlockSpec(a.shape, lambda b, i: (0,) * a.ndim)
    return pl.pallas_call(
        _front_kernel,
        grid=(B, S // tm),
        in_specs=[
            pl.BlockSpec((1, tm, D), lambda b, i: (b, i, 0)),
            pl.BlockSpec((1, MLA_ROPE, tm), lambda b, i: (b, 0, i)),
            pl.BlockSpec((1, MLA_ROPE, tm), lambda b, i: (b, 0, i)),
            pl.BlockSpec((1, tm, LANES), lambda b, i: (b, i, 0)),
            pl.BlockSpec((1, tm, LANES), lambda b, i: (b, i, 0)),
        ] + [full(w) for w in weights],
        out_specs=[
            pl.BlockSpec((1, tm, LRU_WIDTH), lambda b, i: (b, i, 0)),
            pl.BlockSpec((1, MLA_HEADS, QK_PAD, tm), lambda b, i: (b, 0, 0, i)),
            pl.BlockSpec((1, MLA_HEADS, tm, QK_PAD), lambda b, i: (b, 0, i, 0)),
            pl.BlockSpec((1, MLA_HEADS, MLA_V, tm), lambda b, i: (b, 0, 0, i)),
            pl.BlockSpec((1, MLA_WIDTH, tm), lambda b, i: (b, 0, i)),
            pl.BlockSpec((1, tm, SGU_WIDTH), lambda b, i: (b, i, 0)),
        ],
        out_shape=[
            jax.ShapeDtypeStruct((B, S, LRU_WIDTH), BF16),
            jax.ShapeDtypeStruct((B, MLA_HEADS, QK_PAD, S), BF16),
            jax.ShapeDtypeStruct((B, MLA_HEADS, S, QK_PAD), BF16),
            jax.ShapeDtypeStruct((B, MLA_HEADS, MLA_V, S), BF16),
            jax.ShapeDtypeStruct((B, MLA_WIDTH, S), BF16),
            jax.ShapeDtypeStruct((B, S, SGU_WIDTH), BF16),
        ],
        scratch_shapes=[
            pltpu.VMEM((tm + SUBLANES, LRU_WIDTH), F32),
            pltpu.VMEM((1, LRU_WIDTH), F32),
            pltpu.VMEM((tm, LRU_WIDTH), F32),
            pltpu.VMEM((tm, LRU_WIDTH), F32),
        ],
        compiler_params=_cparams("parallel", "arbitrary"),
        name="front",
    )(x, ct, st, c128, s128, *weights)


def _attn_kernel(qt_ref, k_ref, vt_ref, o_ref, s_even, s_odd, m_scr):
    i = pl.program_id(2)
    nq = pl.num_programs(2) - 1
    tq = s_even.shape[1]
    tk = tq
    wslot = i % 2
    rslot = 1 - wslot

    @pl.when(i == 0)
    def _():
        m_scr[...] = jnp.zeros_like(m_scr)

    qs = pl.multiple_of(jnp.minimum(i, nq - 1) * tq, tq)
    qt = qt_ref[0, 0, :, pl.ds(qs, tq)]
    m_prev = m_scr[rslot]

    def fold(x):
        return x.reshape(x.shape[0] // SUBLANES, SUBLANES, tq)

    def scores(ks):
        k = k_ref[0, 0, pl.ds(ks, tk), :]
        return jnp.dot(k, qt, preferred_element_type=F32)

    def run(s_w, s_r):
        def score_tile(ks, mrun):
            s = scores(ks)
            s_w[pl.ds(ks, tk), :] = s
            return jnp.maximum(mrun, jnp.max(fold(s), axis=0))

        def finish_tile(ks, lrun, acc):
            p = jnp.exp2(s_r[pl.ds(ks, tk), :] - m_prev)
            lrun = lrun + jnp.sum(fold(p), axis=0)
            v = vt_ref[0, 0, :, pl.ds(ks, tk)]
            acc = acc + jnp.dot(v, p.astype(BF16), preferred_element_type=F32)
            return lrun, acc

        def both(nsub):
            def body(j, carry):
                mrun, lrun, acc = carry
                for u in range(nsub):
                    ks = pl.multiple_of((j * nsub + u) * tk, tk)
                    mrun = score_tile(ks, mrun)
                    lrun, acc = finish_tile(ks, lrun, acc)
                return mrun, lrun, acc
            return body

        def finish_only(nsub):
            def body(j, carry):
                lrun, acc = carry
                for u in range(nsub):
                    ks = pl.multiple_of((j * nsub + u) * tk, tk)
                    lrun, acc = finish_tile(ks, lrun, acc)
                return lrun, acc
            return body

        def score_diag(mrun):
            ks = pl.multiple_of(i * tk, tk)
            s = scores(ks)
            kc = lax.broadcasted_iota(jnp.int32, s.shape, 0) // CHUNK
            qc = lax.broadcasted_iota(jnp.int32, s.shape, 1) // CHUNK
            s = jnp.where(kc <= qc, s, -jnp.inf)
            s_w[pl.ds(ks, tk), :] = s
            return jnp.maximum(mrun, jnp.max(fold(s), axis=0))

        scoring = i < nq
        paired = jnp.logical_and(scoring, i > 0)
        n_big = jnp.where(paired, (i - 1) // ATT_KBIG, 0)
        n_rest = (i - 1) % ATT_KBIG + 1

        def tail(ntile):
            def body(_, carry):
                mrun, lrun, acc = carry
                for u in range(ntile):
                    ks = pl.multiple_of((n_big * ATT_KBIG + u) * tk, tk)
                    mrun = score_tile(ks, mrun)
                    if u == 0:
                        mrun = score_diag(mrun)
                    lrun, acc = finish_tile(ks, lrun, acc)
                return mrun, lrun, acc
            return body

        carry = (jnp.full((SUBLANES, tq), -jnp.inf, F32), jnp.zeros((SUBLANES, tq), F32),
                 jnp.zeros((MLA_V, tq), F32))
        carry = lax.fori_loop(0, n_big, both(ATT_KBIG), carry)
        for ntile in range(1, ATT_KBIG + 1):
            run_it = jnp.logical_and(paired, n_rest == ntile)
            carry = lax.fori_loop(0, run_it.astype(jnp.int32), tail(ntile), carry)
        mrun, lrun, acc = carry
        n_only = jnp.where(scoring, 0, i)
        n_big = n_only // ATT_KBIG
        carry = lax.fori_loop(0, n_big, finish_only(ATT_KBIG), (lrun, acc))
        lrun, acc = lax.fori_loop(n_big * ATT_KBIG, n_only, finish_only(1), carry)

        @pl.when(i == 0)
        def _():
            m_scr[wslot] = jnp.max(score_diag(mrun), axis=0, keepdims=True)

        @pl.when(paired)
        def _():
            m_scr[wslot] = jnp.max(mrun, axis=0, keepdims=True)

        return lrun, acc

    lrun, acc = lax.cond(wslot == 0, lambda: run(s_even, s_odd), lambda: run(s_odd, s_even))

    @pl.when(i > 0)
    def _():
        l = jnp.sum(lrun, axis=0, keepdims=True)
        os = pl.multiple_of((i - 1) * tq, tq)
        o_ref[0, 0, :, pl.ds(os, tq)] = (acc / l).astype(o_ref.dtype)


def _attention(qt, k, vt, tq):
    B, H, _, S = qt.shape
    nq = S // tq
    return pl.pallas_call(
        _attn_kernel,
        grid=(B, H, nq + 1),
        in_specs=[
            pl.BlockSpec((1, 1, QK_PAD, S), lambda b, h, i: (b, h, 0, 0)),
            pl.BlockSpec((1, 1, S, QK_PAD), lambda b, h, i: (b, h, 0, 0)),
            pl.BlockSpec((1, 1, MLA_V, S), lambda b, h, i: (b, h, 0, 0)),
        ],
        out_specs=pl.BlockSpec((1, 1, MLA_V, S), lambda b, h, i: (b, h, 0, 0)),
        out_shape=jax.ShapeDtypeStruct((B, H, MLA_V, S), BF16),
        scratch_shapes=[pltpu.VMEM((S, tq), F32), pltpu.VMEM((S, tq), F32),
                        pltpu.VMEM((2, 1, tq), F32)],
        compiler_params=pltpu.CompilerParams(
            dimension_semantics=("parallel", "parallel", "arbitrary"),
            vmem_limit_bytes=ATT_VMEM_LIMIT),
        name="attention",
    )(qt, k, vt)


def _outproj_kernel(x_ref, ya_ref, ot_ref, gbt_ref, yc_ref, bngb_ref, w_ref, pg_ref, out_ref):
    gbt = gbt_ref[0].astype(F32)
    ybt = ot_ref[0].astype(F32) * _silu(gbt)
    ms = jnp.mean(ybt * ybt, axis=0, keepdims=True)
    yb = (ybt * lax.rsqrt(ms + EPS) * bngb_ref[...]).T.astype(BF16)
    y = jnp.dot(jnp.concatenate([ya_ref[0], yb, yc_ref[0]], axis=1), w_ref[...],
                preferred_element_type=F32)
    ms = jnp.mean(y * y, axis=-1, keepdims=True)
    out_ref[0] = x_ref[0] + y * lax.rsqrt(ms + EPS) * pg_ref[...]


def _outproj(x, ya, ot, gbt, yc, bng_b, w_out, post_g, tm):
    B, S, D = x.shape
    full = lambda shape: pl.BlockSpec(shape, lambda b, i: (0,) * len(shape))
    return pl.pallas_call(
        _outproj_kernel,
        grid=(B, S // tm),
        in_specs=[
            pl.BlockSpec((1, tm, D), lambda b, i: (b, i, 0)),
            pl.BlockSpec((1, tm, LRU_WIDTH), lambda b, i: (b, i, 0)),
            pl.BlockSpec((1, MLA_WIDTH, tm), lambda b, i: (b, 0, i)),
            pl.BlockSpec((1, MLA_WIDTH, tm), lambda b, i: (b, 0, i)),
            pl.BlockSpec((1, tm, SGU_WIDTH), lambda b, i: (b, i, 0)),
            full((MLA_WIDTH, 1)),
            full(w_out.shape),
            full((1, D)),
        ],
        out_specs=pl.BlockSpec((1, tm, D), lambda b, i: (b, i, 0)),
        out_shape=jax.ShapeDtypeStruct((B, S, D), F32),
        compiler_params=_cparams("parallel", "parallel"),
        name="outproj",
    )(x, ya, ot, gbt, yc, bng_b, w_out, post_g)


def _layer_params(l, pre_norm_g, w_in, conv_w, conv_b, lru_wa, lru_ba, lru_wx, lru_bx, lru_lambda,
                  q_norm_g, w_uq, kv_norm_g, w_ukv, sgu_norm_g, sgu_norm_b, sgu_w, sgu_b,
                  branch_norm_g, w_out, post_norm_g):
    D = w_in.shape[1]
    offs = np.cumsum([0, LRU_WIDTH, LRU_WIDTH, Q_RANK, KV_RANK, MLA_ROPE, MLA_WIDTH,
                      SGU_WIDTH, SGU_WIDTH, SGU_WIDTH])
    win = w_in[l]
    cols = lambda i: win[:, offs[i]:offs[i + 1]]
    zeros = lambda n: jnp.zeros((D, n), win.dtype)
    w_tok = jnp.concatenate([
        cols(0), cols(1),
        cols(2), cols(4), zeros(Q_LAT_PAD - Q_RANK - MLA_ROPE), cols(3),
        cols(6), cols(7), cols(8),
    ], axis=1).astype(BF16)
    w_gbt = cols(5).T.astype(BF16)

    def block_diag(w):
        eye = jnp.eye(LRU_HEADS, dtype=w.dtype)
        return jnp.einsum('hij,hg->higj', w, eye).reshape(LRU_WIDTH, LRU_WIDTH)

    w_gate = jnp.concatenate([block_diag(lru_wa[l]), block_diag(lru_wx[l])], axis=1).astype(BF16)
    b_gate = jnp.concatenate([lru_ba[l], lru_bx[l]]).reshape(1, -1)

    q_g = jnp.concatenate([q_norm_g[l], jnp.zeros((Q_LAT_PAD - Q_RANK,), F32)]).reshape(1, -1)
    wqt = jnp.concatenate([w_uq[l], jnp.zeros((Q_LAT_PAD - Q_RANK, w_uq.shape[2]), F32)], axis=0)
    wqt = wqt.T.astype(BF16)
    wkv = w_ukv[l].reshape(KV_RANK, MLA_HEADS, MLA_NOPE + MLA_V)
    wk = jnp.concatenate([wkv[:, :, :MLA_NOPE],
                          jnp.zeros((KV_RANK, MLA_HEADS, LANES - MLA_NOPE), F32)], axis=2)
    wk = wk.reshape(KV_RANK, MLA_HEADS * LANES).astype(BF16)
    wvt = wkv[:, :, MLA_NOPE:].reshape(KV_RANK, MLA_HEADS * MLA_V).T.astype(BF16)

    w_stack = sgu_w[l].reshape(SGU_GROUPS * SGU_BLOCK, SGU_BLOCK)
    sgu_bias = jnp.repeat(sgu_b[l].T, SGU_GROUP_DIM, axis=1)

    bng = branch_norm_g[l]
    wo = w_out[l].astype(BF16)
    return dict(
        pre_g=pre_norm_g[l].reshape(1, -1), w_tok=w_tok, w_gbt=w_gbt,
        conv_w=conv_w[l], conv_b=conv_b[l].reshape(1, -1), w_gate=w_gate, b_gate=b_gate,
        lam=lru_lambda[l].reshape(1, -1),
        q_g=q_g, kv_g=kv_norm_g[l].reshape(1, -1), wqt=wqt, wk=wk, wvt=wvt,
        sgu_g=sgu_norm_g[l].reshape(1, -1), sgu_nb=sgu_norm_b[l].reshape(1, -1),
        w_stack=w_stack, sgu_bias=sgu_bias,
        bng_a=bng[:LRU_WIDTH].reshape(1, -1),
        bng_b=bng[LRU_WIDTH:LRU_WIDTH + MLA_WIDTH].reshape(-1, 1),
        bng_c=bng[LRU_WIDTH + MLA_WIDTH:].reshape(1, -1),
        w_out=wo, post_g=post_norm_g[l].reshape(1, -1),
    )


def kernel(x, positions, pre_norm_g, w_in, conv_w, conv_b, lru_wa, lru_ba, lru_wx, lru_bx, lru_lambda,
           q_norm_g, w_uq, kv_norm_g, w_ukv, sgu_norm_g, sgu_norm_b, sgu_w, sgu_b, branch_norm_g,
           w_out, post_norm_g):
    B, S, D = x.shape
    depth = w_in.shape[0]
    tm = min(ROW_TILE, S)
    tq = min(ATT_TQ, S)
    to = min(OUT_TILE, S)
    assert S % tm == 0 and S % tq == 0 and S % to == 0 and tq % CHUNK == 0 and tm % SGU_BLOCK == 0

    ct, st, c128, s128 = _rope_tables(positions, tm)
    h = x
    for l in range(depth):
        p = _layer_params(l, pre_norm_g, w_in, conv_w, conv_b, lru_wa, lru_ba, lru_wx, lru_bx,
                          lru_lambda, q_norm_g, w_uq, kv_norm_g, w_ukv, sgu_norm_g, sgu_norm_b,
                          sgu_w, sgu_b, branch_norm_g, w_out, post_norm_g)
        ya, qt, k, vt, gbt, yc = _front(h, ct, st, c128, s128, p, tm)
        ot = _attention(qt, k, vt, tq).reshape(B, MLA_WIDTH, S)
        h = _outproj(h, ya, ot, gbt, yc, p['bng_b'], p['w_out'], p['post_g'], to)
    return h
```

```python
import math

import numpy as np
import jax
import jax.numpy as jnp
from jax import lax
from jax.experimental import pallas as pl
from jax.experimental.pallas import tpu as pltpu

F32 = jnp.float32
BF16 = jnp.bfloat16

EPS = 1e-6
CHUNK = 64

LRU_WIDTH = 384
LRU_HEADS = 6
LRU_HEAD_DIM = 64
CONV_WIDTH = 4
LRU_C = 8.0

MLA_HEADS = 6
MLA_NOPE = 64
MLA_ROPE = 32
MLA_V = 64
MLA_QK = MLA_NOPE + MLA_ROPE
MLA_WIDTH = MLA_HEADS * MLA_V
Q_RANK = 192
KV_RANK = 128
ROPE_THETA = 10000.0

SGU_WIDTH = 256
SGU_GROUPS = 4
SGU_GROUP_DIM = 64
SGU_BLOCK = 128

LANES = 128
SUBLANES = 8
QK_PAD = 128
V_PAD = 80
Q_LAT_PAD = 256

COL_A = 0
COL_LAT = COL_A + 2 * LRU_WIDTH
COL_C = COL_LAT + Q_LAT_PAD + KV_RANK
N_TOK = COL_C + 3 * SGU_WIDTH
assert (Q_RANK - LANES) == MLA_NOPE and Q_RANK + MLA_ROPE <= Q_LAT_PAD

ROW_TILE = 512
OUT_TILE = 1024
ATT_TQ = 512
ATT_KBIG = 8
VMEM_LIMIT = 48 * 1024 * 1024

NT_DIMS = (((1,), (1,)), ((), ()))
TN_DIMS = (((0,), (0,)), ((), ()))


def _cparams(*sem):
    return pltpu.CompilerParams(dimension_semantics=sem, vmem_limit_bytes=VMEM_LIMIT)


def _silu(x):
    return x * jax.nn.sigmoid(x)


def _gelu(x):
    c = math.sqrt(2.0 / math.pi)
    return 0.5 * x * (1.0 + jnp.tanh(c * (x + 0.044715 * (x * x * x))))


def _rope_table_kernel(pos_ref, inv_ref, ct_ref, st_ref, c128_ref, s128_ref):
    pos = pos_ref[0].astype(F32)
    ang = inv_ref[...] * pos
    cos = jnp.cos(ang)
    sin = jnp.sin(ang)
    row = lax.broadcasted_iota(jnp.int32, ang.shape, 0)
    sin_signed = jnp.where(row < MLA_ROPE // 2, -sin, sin)
    ct_ref[0] = cos
    st_ref[0] = sin_signed
    ts = ang.shape[1]
    zeros_lo = jnp.zeros((MLA_NOPE, ts), F32)
    zeros_hi = jnp.zeros((LANES - MLA_QK, ts), F32)
    c128_ref[0] = jnp.concatenate([zeros_lo, cos, zeros_hi], axis=0).T
    s128_ref[0] = jnp.concatenate([zeros_lo, sin_signed, zeros_hi], axis=0).T


def _rope_tables(positions, ts):
    B, S = positions.shape
    half = MLA_ROPE // 2
    inv_freq = ROPE_THETA ** (-jnp.arange(half, dtype=F32) / half)
    inv2 = jnp.concatenate([inv_freq, inv_freq]).reshape(MLA_ROPE, 1)
    pos3 = positions.reshape(B, 1, S)
    return pl.pallas_call(
        _rope_table_kernel,
        grid=(B, S // ts),
        in_specs=[
            pl.BlockSpec((1, 1, ts), lambda b, i: (b, 0, i)),
            pl.BlockSpec((MLA_ROPE, 1), lambda b, i: (0, 0)),
        ],
        out_specs=[
            pl.BlockSpec((1, MLA_ROPE, ts), lambda b, i: (b, 0, i)),
            pl.BlockSpec((1, MLA_ROPE, ts), lambda b, i: (b, 0, i)),
            pl.BlockSpec((1, ts, LANES), lambda b, i: (b, i, 0)),
            pl.BlockSpec((1, ts, LANES), lambda b, i: (b, i, 0)),
        ],
        out_shape=[
            jax.ShapeDtypeStruct((B, MLA_ROPE, S), F32),
            jax.ShapeDtypeStruct((B, MLA_ROPE, S), F32),
            jax.ShapeDtypeStruct((B, S, LANES), F32),
            jax.ShapeDtypeStruct((B, S, LANES), F32),
        ],
        compiler_params=_cparams("parallel", "parallel"),
        name="rope_tables",
    )(pos3, inv2)


def _scan8(a, b):
    row = lax.broadcasted_iota(jnp.int32, a.shape, 0)
    for d in (1, 2, 4):
        a_prev = pltpu.roll(a, d, 0)
        b_prev = pltpu.roll(b, d, 0)
        live = row >= d
        b = jnp.where(live, a * b_prev + b, b)
        a = jnp.where(live, a * a_prev, a)
    return a, b


def _rglru_inputs(a_in, convw_ref, convb_ref, wg_ref, bg_ref, lam_ref, xbuf, abuf, bbuf):
    ts = a_in.shape[0]
    halo = SUBLANES
    xa = a_in[:, 0:LRU_WIDTH]
    ga = a_in[:, LRU_WIDTH:2 * LRU_WIDTH]
    xbuf[halo:halo + ts, :] = xa
    xc = convb_ref[...]
    for k in range(CONV_WIDTH):
        off = halo - (CONV_WIDTH - 1) + k
        xc = xc + convw_ref[k:k + 1, :] * xbuf[off:off + ts, :]
    xbuf[0:halo, :] = xbuf[ts:ts + halo, :]

    gz = jnp.dot(xc.astype(BF16), wg_ref[...], preferred_element_type=F32) + bg_ref[...]
    gate_a = jax.nn.sigmoid(gz[:, 0:LRU_WIDTH])
    gate_x = jax.nn.sigmoid(gz[:, LRU_WIDTH:2 * LRU_WIDTH])
    nl = -lam_ref[...]
    softplus = jnp.maximum(nl, 0.0) + jnp.log(1.0 + jnp.exp(-jnp.abs(nl)))
    log_a = (-LRU_C) * gate_a * softplus
    a = jnp.exp(log_a)
    mult = jnp.sqrt(1.0 - a * a)
    abuf[...] = a
    bbuf[...] = mult * (gate_x * xc)
    return _silu(ga)


def _rglru_scan(hcar, abuf, bbuf):
    def blk(i, h):
        r = pl.multiple_of(i * SUBLANES, SUBLANES)
        a8, b8 = _scan8(abuf[pl.ds(r, SUBLANES), :], bbuf[pl.ds(r, SUBLANES), :])
        rows = a8 * h + b8
        bbuf[pl.ds(r, SUBLANES), :] = rows
        return rows[SUBLANES - 1:SUBLANES, :]

    hcar[...] = lax.fori_loop(0, abuf.shape[0] // SUBLANES, blk, hcar[...])


def _rglru_output(g_silu, bbuf, bng_ref):
    ya = bbuf[...] * g_silu
    ms = jnp.mean(ya * ya, axis=-1, keepdims=True)
    return ya * lax.rsqrt(ms + EPS) * bng_ref[...]


def _mla_prep(lat, ct, st, c128, s128, qg_ref, kvg_ref, wqt_ref, wk_ref, wvt_ref,
              qt_ref, k_ref, vt_ref):
    ts = lat.shape[0]
    q_blk = lat[:, 0:Q_LAT_PAD]
    kv_lat = lat[:, Q_LAT_PAD:Q_LAT_PAD + KV_RANK]
    q_lane = lax.broadcasted_iota(jnp.int32, q_blk.shape, 1)
    q_only = jnp.where(q_lane < Q_RANK, q_blk, 0.0)
    q_ms = jnp.sum(q_only * q_only, axis=-1, keepdims=True) * (1.0 / Q_RANK)
    qn = (q_blk * lax.rsqrt(q_ms + EPS) * qg_ref[...]).astype(BF16)
    kv_ms = jnp.mean(kv_lat * kv_lat, axis=-1, keepdims=True)
    kvn = (kv_lat * lax.rsqrt(kv_ms + EPS) * kvg_ref[...]).astype(BF16)

    qt = lax.dot_general(wqt_ref[...], qn, NT_DIMS, preferred_element_type=F32)
    half = MLA_ROPE // 2
    q_scale = (MLA_QK ** -0.5) * math.log2(math.e)
    pad = jnp.zeros((QK_PAD - MLA_QK, ts), F32)
    for h in range(MLA_HEADS):
        base = h * MLA_QK
        nope = qt[base:base + MLA_NOPE, :]
        pe = qt[base + MLA_NOPE:base + MLA_QK, :]
        pe_sw = jnp.concatenate([pe[half:, :], pe[:half, :]], axis=0)
        pe = pe * ct + pe_sw * st
        qh = jnp.concatenate([nope, pe, pad], axis=0) * q_scale
        qt_ref[0, h] = qh.astype(qt_ref.dtype)

    kn = jnp.dot(kvn, wk_ref[...], preferred_element_type=F32)
    kr = lat[:, LANES:2 * LANES]
    half_swapped = jnp.where(lax.broadcasted_iota(jnp.int32, kr.shape, 1) < MLA_NOPE + half,
                             pltpu.roll(kr, LANES - half, 1), pltpu.roll(kr, half, 1))
    pe128 = kr * c128 + half_swapped * s128
    for h in range(MLA_HEADS):
        k_ref[0, h] = (kn[:, h * LANES:(h + 1) * LANES] + pe128).astype(k_ref.dtype)

    vt = lax.dot_general(wvt_ref[...], kvn, NT_DIMS, preferred_element_type=F32)
    row = lax.broadcasted_iota(jnp.int32, (V_PAD - MLA_V, ts), 0)
    ones_row = jnp.where(row == 0, 1.0, 0.0).astype(F32)
    for h in range(MLA_HEADS):
        vh = jnp.concatenate([vt[h * MLA_V:(h + 1) * MLA_V, :], ones_row], axis=0)
        vt_ref[0, h] = vh.astype(vt_ref.dtype)


def _sgu_branch(c_in, ng_ref, nb_ref, w_ref, bias_ref, bng_ref):
    tm = c_in.shape[0]
    u = _gelu(c_in[:, 0:SGU_WIDTH])
    v = _gelu(c_in[:, SGU_WIDTH:2 * SGU_WIDTH])
    gc = c_in[:, 2 * SGU_WIDTH:3 * SGU_WIDTH]
    mu = jnp.mean(v, axis=-1, keepdims=True)
    vc = v - mu
    var = jnp.mean(vc * vc, axis=-1, keepdims=True)
    vn = (vc * lax.rsqrt(var + EPS) * ng_ref[...] + nb_ref[...]).astype(BF16)

    w = w_ref[...]
    wi = (lax.broadcasted_iota(jnp.int32, w.shape, 0) % SGU_BLOCK) // CHUNK
    wj = lax.broadcasted_iota(jnp.int32, w.shape, 1) // CHUNK
    w = jnp.where(wi >= wj, w, 0.0).astype(BF16)

    lane_group = lax.broadcasted_iota(jnp.int32, (SGU_BLOCK, SGU_WIDTH), 1) // SGU_GROUP_DIM
    mixed = []
    for n in range(tm // SGU_BLOCK):
        r = jnp.dot(w, vn[n * SGU_BLOCK:(n + 1) * SGU_BLOCK, :], preferred_element_type=F32)
        m = r[0:SGU_BLOCK, :]
        for g in range(1, SGU_GROUPS):
            m = jnp.where(lane_group == g, r[g * SGU_BLOCK:(g + 1) * SGU_BLOCK, :], m)
        mixed.append(m + bias_ref[...])
    mixed = jnp.concatenate(mixed, axis=0)

    yc = u * mixed * _silu(gc)
    ms = jnp.mean(yc * yc, axis=-1, keepdims=True)
    return yc * lax.rsqrt(ms + EPS) * bng_ref[...]


def _front_kernel(x_ref, ct_ref, st_ref, c128_ref, s128_ref,
                  preg_ref, wtok_ref, wgbt_ref,
                  convw_ref, convb_ref, wg_ref, bg_ref, lam_ref, bnga_ref,
                  qg_ref, kvg_ref, wqt_ref, wk_ref, wvt_ref,
                  sng_ref, snb_ref, sw_ref, sbias_ref, bngc_ref,
                  ya_ref, qt_ref, k_ref, vt_ref, gbt_ref, yc_ref,
                  xbuf, hcar, abuf, bbuf):
    @pl.when(pl.program_id(1) == 0)
    def _():
        xbuf[0:SUBLANES, :] = jnp.zeros((SUBLANES, LRU_WIDTH), F32)
        hcar[...] = jnp.zeros_like(hcar)

    x = x_ref[0]
    ms = jnp.mean(x * x, axis=-1, keepdims=True)
    h = (x * lax.rsqrt(ms + EPS) * preg_ref[...]).astype(BF16)

    def proj(lo, hi):
        return jnp.dot(h, wtok_ref[:, lo:hi], preferred_element_type=F32)

    g_silu = _rglru_inputs(proj(COL_A, COL_LAT), convw_ref, convb_ref, wg_ref, bg_ref, lam_ref,
                           xbuf, abuf, bbuf)
    lat = proj(COL_LAT, COL_C)
    yc = _sgu_branch(proj(COL_C, N_TOK), sng_ref, snb_ref, sw_ref, sbias_ref, bngc_ref)
    yc_ref[0] = yc.astype(yc_ref.dtype)
    gbt = lax.dot_general(wgbt_ref[...], h, NT_DIMS, preferred_element_type=F32)
    gbt_ref[0] = gbt.astype(gbt_ref.dtype)

    _rglru_scan(hcar, abuf, bbuf)

    ya_ref[0] = _rglru_output(g_silu, bbuf, bnga_ref).astype(ya_ref.dtype)
    _mla_prep(lat, ct_ref[0], st_ref[0], c128_ref[0], s128_ref[0], qg_ref, kvg_ref,
              wqt_ref, wk_ref, wvt_ref, qt_ref, k_ref, vt_ref)


def _front(x, ct, st, c128, s128, p, tm):
    B, S, D = x.shape
    weights = [p[n] for n in (
        'pre_g', 'w_tok', 'w_gbt',
        'conv_w', 'conv_b', 'w_gate', 'b_gate', 'lam', 'bng_a',
        'q_g', 'kv_g', 'wqt', 'wk', 'wvt',
        'sgu_g', 'sgu_nb', 'w_stack', 'sgu_bias', 'bng_c')]
    full = lambda a: pl.BlockSpec(a.shape, lambda b, i: (0,) * a.ndim)
    return pl.pallas_call(
        _front_kernel,
        grid=(B, S // tm),
        in_specs=[
            pl.BlockSpec((1, tm, D), lambda b, i: (b, i, 0)),
            pl.BlockSpec((1, MLA_ROPE, tm), lambda b, i: (b, 0, i)),
            pl.BlockSpec((1, MLA_ROPE, tm), lambda b, i: (b, 0, i)),
            pl.BlockSpec((1, tm, LANES), lambda b, i: (b, i, 0)),
            pl.BlockSpec((1, tm, LANES), lambda b, i: (b, i, 0)),
        ] + [full(w) for w in weights],
        out_specs=[
            pl.BlockSpec((1, tm, LRU_WIDTH), lambda b, i: (b, i, 0)),
            pl.BlockSpec((1, MLA_HEADS, QK_PAD, tm), lambda b, i: (b, 0, 0, i)),
            pl.BlockSpec((1, MLA_HEADS, tm, QK_PAD), lambda b, i: (b, 0, i, 0)),
            pl.BlockSpec((1, MLA_HEADS, V_PAD, tm), lambda b, i: (b, 0, 0, i)),
            pl.BlockSpec((1, MLA_WIDTH, tm), lambda b, i: (b, 0, i)),
            pl.BlockSpec((1, tm, SGU_WIDTH), lambda b, i: (b, i, 0)),
        ],
        out_shape=[
            jax.ShapeDtypeStruct((B, S, LRU_WIDTH), BF16),
            jax.ShapeDtypeStruct((B, MLA_HEADS, QK_PAD, S), BF16),
            jax.ShapeDtypeStruct((B, MLA_HEADS, S, QK_PAD), BF16),
            jax.ShapeDtypeStruct((B, MLA_HEADS, V_PAD, S), BF16),
            jax.ShapeDtypeStruct((B, MLA_WIDTH, S), BF16),
            jax.ShapeDtypeStruct((B, S, SGU_WIDTH), BF16),
        ],
        scratch_shapes=[
            pltpu.VMEM((tm + SUBLANES, LRU_WIDTH), F32),
            pltpu.VMEM((1, LRU_WIDTH), F32),
            pltpu.VMEM((tm, LRU_WIDTH), F32),
            pltpu.VMEM((tm, LRU_WIDTH), F32),
        ],
        compiler_params=_cparams("parallel", "arbitrary"),
        name="front",
    )(x, ct, st, c128, s128, *weights)


def _attn_kernel(qt_ref, k_ref, vt_ref, o_ref, s_0, s_1, s_2, s_3, acc_scr, m_scr):
    s_bufs = (s_0, s_1, s_2, s_3)
    S = k_ref.shape[2]
    tk, tq = s_0.shape
    nq = S // tq
    n_off = nq * (nq - 1) // 2
    assert tk == tq and ATT_KBIG % len(s_bufs) == 0
    assert n_off % ATT_KBIG == 0 and nq % ATT_KBIG == 0

    m_scr[...] = jnp.full(m_scr.shape, -jnp.inf, F32)
    acc_scr[...] = jnp.zeros_like(acc_scr)

    def fold(x):
        return x.reshape(x.shape[0] // SUBLANES, SUBLANES, tq)

    def score(j, i, s_buf, diagonal):
        jc = jnp.minimum(j, nq - 1)
        ic = jnp.minimum(i, nq - 1)
        ks = pl.multiple_of(jc * tk, tk)
        qs = pl.multiple_of(ic * tq, tq)
        s = jnp.dot(k_ref[0, 0, pl.ds(ks, tk), :], qt_ref[0, 0, :, pl.ds(qs, tq)],
                    preferred_element_type=F32)
        if diagonal:
            kc = lax.broadcasted_iota(jnp.int32, s.shape, 0) // CHUNK
            qc = lax.broadcasted_iota(jnp.int32, s.shape, 1) // CHUNK
            s = jnp.where(kc <= qc, s, -jnp.inf)
        s_buf[...] = s
        return jnp.max(jnp.max(fold(s), axis=0), axis=0, keepdims=True)

    def finish(j, i, s_buf, mx):
        ks = pl.multiple_of(j * tk, tk)
        qs = pl.multiple_of(i * tq, tq)
        m_old = m_scr[:, pl.ds(qs, tq)]
        m_new = jnp.maximum(m_old, mx)
        alpha = jnp.exp2(m_old - m_new)
        p = jnp.exp2(s_buf[...] - m_new).astype(BF16)
        pv = jnp.dot(vt_ref[0, 0, :, pl.ds(ks, tk)], p, preferred_element_type=F32)
        acc_scr[:, pl.ds(qs, tq)] = alpha * acc_scr[:, pl.ds(qs, tq)] + pv
        m_scr[:, pl.ds(qs, tq)] = m_new

    def run(n_tiles, first, advance, diagonal):
        nbuf = len(s_bufs)

        def body(_, carry):
            (j0, i0, mx0), (j1, i1, mx1) = carry
            for u in range(ATT_KBIG):
                j2, i2 = advance(j1, i1)
                mx2 = score(j2, i2, s_bufs[(u + 2) % nbuf], diagonal)
                finish(j0, i0, s_bufs[u % nbuf], mx0)
                (j0, i0, mx0), (j1, i1, mx1) = (j1, i1, mx1), (j2, i2, mx2)
            return (j0, i0, mx0), (j1, i1, mx1)

        j0, i0 = first
        j1, i1 = advance(j0, i0)
        t0 = (j0, i0, score(j0, i0, s_bufs[0], diagonal))
        t1 = (j1, i1, score(j1, i1, s_bufs[1], diagonal))
        lax.fori_loop(0, n_tiles // ATT_KBIG, body, (t0, t1))

    def next_off_diagonal(j, i):
        wrap = i == nq - 1
        return jnp.where(wrap, j + 1, j), jnp.where(wrap, j + 2, i + 1)

    run(n_off, (jnp.int32(0), jnp.int32(1)), next_off_diagonal, False)
    run(nq, (jnp.int32(0), jnp.int32(0)), lambda j, i: (j + 1, i + 1), True)

    l = acc_scr[MLA_V:MLA_V + 1, :]
    o_ref[0, 0] = (acc_scr[0:MLA_V, :] / l).astype(o_ref.dtype)


def _attention(qt, k, vt, tq):
    B, H, _, S = qt.shape
    return pl.pallas_call(
        _attn_kernel,
        grid=(B, H),
        in_specs=[
            pl.BlockSpec((1, 1, QK_PAD, S), lambda b, h: (b, h, 0, 0)),
            pl.BlockSpec((1, 1, S, QK_PAD), lambda b, h: (b, h, 0, 0)),
            pl.BlockSpec((1, 1, V_PAD, S), lambda b, h: (b, h, 0, 0)),
        ],
        out_specs=pl.BlockSpec((1, 1, MLA_V, S), lambda b, h: (b, h, 0, 0)),
        out_shape=jax.ShapeDtypeStruct((B, H, MLA_V, S), BF16),
        scratch_shapes=[pltpu.VMEM((tq, tq), F32)] * 4 + [
                        pltpu.VMEM((V_PAD, S), F32), pltpu.VMEM((1, S), F32)],
        compiler_params=_cparams("parallel", "parallel"),
        name="attention",
    )(qt, k, vt)


def _outproj_kernel(x_ref, ya_ref, ot_ref, gbt_ref, yc_ref, bngb_ref, w_ref, pg_ref, out_ref):
    gbt = gbt_ref[0].astype(F32)
    ybt = ot_ref[0].astype(F32) * _silu(gbt)
    ms = jnp.mean(ybt * ybt, axis=0, keepdims=True)
    yb = (ybt * lax.rsqrt(ms + EPS) * bngb_ref[...]).T.astype(BF16)
    y = jnp.dot(jnp.concatenate([ya_ref[0], yb, yc_ref[0]], axis=1), w_ref[...],
                preferred_element_type=F32)
    ms = jnp.mean(y * y, axis=-1, keepdims=True)
    out_ref[0] = x_ref[0] + y * lax.rsqrt(ms + EPS) * pg_ref[...]


def _outproj(x, ya, ot, gbt, yc, bng_b, w_out, post_g, tm):
    B, S, D = x.shape
    full = lambda shape: pl.BlockSpec(shape, lambda b, i: (0,) * len(shape))
    return pl.pallas_call(
        _outproj_kernel,
        grid=(B, S // tm),
        in_specs=[
            pl.BlockSpec((1, tm, D), lambda b, i: (b, i, 0)),
            pl.BlockSpec((1, tm, LRU_WIDTH), lambda b, i: (b, i, 0)),
            pl.BlockSpec((1, MLA_WIDTH, tm), lambda b, i: (b, 0, i)),
            pl.BlockSpec((1, MLA_WIDTH, tm), lambda b, i: (b, 0, i)),
            pl.BlockSpec((1, tm, SGU_WIDTH), lambda b, i: (b, i, 0)),
            full((MLA_WIDTH, 1)),
            full(w_out.shape),
            full((1, D)),
        ],
        out_specs=pl.BlockSpec((1, tm, D), lambda b, i: (b, i, 0)),
        out_shape=jax.ShapeDtypeStruct((B, S, D), F32),
        compiler_params=_cparams("parallel", "parallel"),
        name="outproj",
    )(x, ya, ot, gbt, yc, bng_b, w_out, post_g)


def _layer_params(l, pre_norm_g, w_in, conv_w, conv_b, lru_wa, lru_ba, lru_wx, lru_bx, lru_lambda,
                  q_norm_g, w_uq, kv_norm_g, w_ukv, sgu_norm_g, sgu_norm_b, sgu_w, sgu_b,
                  branch_norm_g, w_out, post_norm_g):
    D = w_in.shape[1]
    offs = np.cumsum([0, LRU_WIDTH, LRU_WIDTH, Q_RANK, KV_RANK, MLA_ROPE, MLA_WIDTH,
                      SGU_WIDTH, SGU_WIDTH, SGU_WIDTH])
    win = w_in[l]
    cols = lambda i: win[:, offs[i]:offs[i + 1]]
    zeros = lambda n: jnp.zeros((D, n), win.dtype)
    w_tok = jnp.concatenate([
        cols(0), cols(1),
        cols(2), cols(4), zeros(Q_LAT_PAD - Q_RANK - MLA_ROPE), cols(3),
        cols(6), cols(7), cols(8),
    ], axis=1).astype(BF16)
    w_gbt = cols(5).T.astype(BF16)

    def block_diag(w):
        eye = jnp.eye(LRU_HEADS, dtype=w.dtype)
        return jnp.einsum('hij,hg->higj', w, eye).reshape(LRU_WIDTH, LRU_WIDTH)

    w_gate = jnp.concatenate([block_diag(lru_wa[l]), block_diag(lru_wx[l])], axis=1).astype(BF16)
    b_gate = jnp.concatenate([lru_ba[l], lru_bx[l]]).reshape(1, -1)

    q_g = jnp.concatenate([q_norm_g[l], jnp.zeros((Q_LAT_PAD - Q_RANK,), F32)]).reshape(1, -1)
    wqt = jnp.concatenate([w_uq[l], jnp.zeros((Q_LAT_PAD - Q_RANK, w_uq.shape[2]), F32)], axis=0)
    wqt = wqt.T.astype(BF16)
    wkv = w_ukv[l].reshape(KV_RANK, MLA_HEADS, MLA_NOPE + MLA_V)
    wk = jnp.concatenate([wkv[:, :, :MLA_NOPE],
                          jnp.zeros((KV_RANK, MLA_HEADS, LANES - MLA_NOPE), F32)], axis=2)
    wk = wk.reshape(KV_RANK, MLA_HEADS * LANES).astype(BF16)
    wvt = wkv[:, :, MLA_NOPE:].reshape(KV_RANK, MLA_HEADS * MLA_V).T.astype(BF16)

    w_stack = sgu_w[l].reshape(SGU_GROUPS * SGU_BLOCK, SGU_BLOCK)
    sgu_bias = jnp.repeat(sgu_b[l].T, SGU_GROUP_DIM, axis=1)

    bng = branch_norm_g[l]
    wo = w_out[l].astype(BF16)
    return dict(
        pre_g=pre_norm_g[l].reshape(1, -1), w_tok=w_tok, w_gbt=w_gbt,
        conv_w=conv_w[l], conv_b=conv_b[l].reshape(1, -1), w_gate=w_gate, b_gate=b_gate,
        lam=lru_lambda[l].reshape(1, -1),
        q_g=q_g, kv_g=kv_norm_g[l].reshape(1, -1), wqt=wqt, wk=wk, wvt=wvt,
        sgu_g=sgu_norm_g[l].reshape(1, -1), sgu_nb=sgu_norm_b[l].reshape(1, -1),
        w_stack=w_stack, sgu_bias=sgu_bias,
        bng_a=bng[:LRU_WIDTH].reshape(1, -1),
        bng_b=bng[LRU_WIDTH:LRU_WIDTH + MLA_WIDTH].reshape(-1, 1),
        bng_c=bng[LRU_WIDTH + MLA_WIDTH:].reshape(1, -1),
        w_out=wo, post_g=post_norm_g[l].reshape(1, -1),
    )


def kernel(x, positions, pre_norm_g, w_in, conv_w, conv_b, lru_wa, lru_ba, lru_wx, lru_bx, lru_lambda,
           q_norm_g, w_uq, kv_norm_g, w_ukv, sgu_norm_g, sgu_norm_b, sgu_w, sgu_b, branch_norm_g,
           w_out, post_norm_g):
    B, S, D = x.shape
    depth = w_in.shape[0]
    tm = min(ROW_TILE, S)
    tq = min(ATT_TQ, S)
    to = min(OUT_TILE, S)
    assert S % tm == 0 and S % tq == 0 and S % to == 0 and tq % CHUNK == 0 and tm % SGU_BLOCK == 0

    ct, st, c128, s128 = _rope_tables(positions, tm)
    h = x
    for l in range(depth):
        p = _layer_params(l, pre_norm_g, w_in, conv_w, conv_b, lru_wa, lru_ba, lru_wx, lru_bx,
                          lru_lambda, q_norm_g, w_uq, kv_norm_g, w_ukv, sgu_norm_g, sgu_norm_b,
                          sgu_w, sgu_b, branch_norm_g, w_out, post_norm_g)
        ya, qt, k, vt, gbt, yc = _front(h, ct, st, c128, s128, p, tm)
        ot = _attention(qt, k, vt, tq).reshape(B, MLA_WIDTH, S)
        h = _outproj(h, ya, ot, gbt, yc, p['bng_b'], p['w_out'], p['post_g'], to)
    return h
```

```python
import math

import numpy as np
import jax
import jax.numpy as jnp
from jax import lax
from jax.experimental import pallas as pl
from jax.experimental.pallas import tpu as pltpu

F32 = jnp.float32
BF16 = jnp.bfloat16

EPS = 1e-6
CHUNK = 64

LRU_WIDTH = 384
LRU_HEADS = 6
LRU_HEAD_DIM = 64
CONV_WIDTH = 4
LRU_C = 8.0

MLA_HEADS = 6
MLA_NOPE = 64
MLA_ROPE = 32
MLA_V = 64
MLA_QK = MLA_NOPE + MLA_ROPE
MLA_WIDTH = MLA_HEADS * MLA_V
Q_RANK = 192
KV_RANK = 128
ROPE_THETA = 10000.0

SGU_WIDTH = 256
SGU_GROUPS = 4
SGU_GROUP_DIM = 64
SGU_BLOCK = 128

LANES = 128
SUBLANES = 8
QK_PAD = 128
V_PAD = 80
Q_LAT_PAD = 256

COL_A = 0
COL_LAT = COL_A + 2 * LRU_WIDTH
COL_C = COL_LAT + Q_LAT_PAD + KV_RANK
N_TOK = COL_C + 3 * SGU_WIDTH
assert (Q_RANK - LANES) == MLA_NOPE and Q_RANK + MLA_ROPE <= Q_LAT_PAD

ROW_TILE = 1024
OUT_TILE = 1024
ATT_TQ = 512
ATT_KBIG = 8
VMEM_LIMIT = 48 * 1024 * 1024

NT_DIMS = (((1,), (1,)), ((), ()))
TN_DIMS = (((0,), (0,)), ((), ()))


def _cparams(*sem):
    return pltpu.CompilerParams(dimension_semantics=sem, vmem_limit_bytes=VMEM_LIMIT)


def _silu(x):
    return x * jax.nn.sigmoid(x)


def _gelu(x):
    c = math.sqrt(2.0 / math.pi)
    return 0.5 * x * (1.0 + jnp.tanh(c * (x + 0.044715 * (x * x * x))))


def _rope_table_kernel(pos_ref, inv_ref, ct_ref, st_ref, c128_ref, s128_ref):
    pos = pos_ref[0].astype(F32)
    ang = inv_ref[...] * pos
    cos = jnp.cos(ang)
    sin = jnp.sin(ang)
    row = lax.broadcasted_iota(jnp.int32, ang.shape, 0)
    sin_signed = jnp.where(row < MLA_ROPE // 2, -sin, sin)
    ct_ref[0] = cos
    st_ref[0] = sin_signed
    ts = ang.shape[1]
    zeros_lo = jnp.zeros((MLA_NOPE, ts), F32)
    zeros_hi = jnp.zeros((LANES - MLA_QK, ts), F32)
    c128_ref[0] = jnp.concatenate([zeros_lo, cos, zeros_hi], axis=0).T
    s128_ref[0] = jnp.concatenate([zeros_lo, sin_signed, zeros_hi], axis=0).T


def _rope_tables(positions, ts):
    B, S = positions.shape
    half = MLA_ROPE // 2
    inv_freq = ROPE_THETA ** (-jnp.arange(half, dtype=F32) / half)
    inv2 = jnp.concatenate([inv_freq, inv_freq]).reshape(MLA_ROPE, 1)
    pos3 = positions.reshape(B, 1, S)
    return pl.pallas_call(
        _rope_table_kernel,
        grid=(B, S // ts),
        in_specs=[
            pl.BlockSpec((1, 1, ts), lambda b, i: (b, 0, i)),
            pl.BlockSpec((MLA_ROPE, 1), lambda b, i: (0, 0)),
        ],
        out_specs=[
            pl.BlockSpec((1, MLA_ROPE, ts), lambda b, i: (b, 0, i)),
            pl.BlockSpec((1, MLA_ROPE, ts), lambda b, i: (b, 0, i)),
            pl.BlockSpec((1, ts, LANES), lambda b, i: (b, i, 0)),
            pl.BlockSpec((1, ts, LANES), lambda b, i: (b, i, 0)),
        ],
        out_shape=[
            jax.ShapeDtypeStruct((B, MLA_ROPE, S), F32),
            jax.ShapeDtypeStruct((B, MLA_ROPE, S), F32),
            jax.ShapeDtypeStruct((B, S, LANES), F32),
            jax.ShapeDtypeStruct((B, S, LANES), F32),
        ],
        compiler_params=_cparams("parallel", "parallel"),
        name="rope_tables",
    )(pos3, inv2)


def _scan8(a, b):
    row = lax.broadcasted_iota(jnp.int32, a.shape, 0)
    for d in (1, 2, 4):
        a_prev = pltpu.roll(a, d, 0)
        b_prev = pltpu.roll(b, d, 0)
        live = row >= d
        b = jnp.where(live, a * b_prev + b, b)
        a = jnp.where(live, a * a_prev, a)
    return a, b


def _rglru_inputs(a_in, convw_ref, convb_ref, wg_ref, bg_ref, lam_ref, xbuf, abuf, bbuf):
    ts = a_in.shape[0]
    halo = SUBLANES
    xa = a_in[:, 0:LRU_WIDTH]
    ga = a_in[:, LRU_WIDTH:2 * LRU_WIDTH]
    xbuf[halo:halo + ts, :] = xa
    xc = convb_ref[...]
    for k in range(CONV_WIDTH):
        off = halo - (CONV_WIDTH - 1) + k
        xc = xc + convw_ref[k:k + 1, :] * xbuf[off:off + ts, :]
    xbuf[0:halo, :] = xbuf[ts:ts + halo, :]

    gz = jnp.dot(xc.astype(BF16), wg_ref[...], preferred_element_type=F32) + bg_ref[...]
    gate_a = jax.nn.sigmoid(gz[:, 0:LRU_WIDTH])
    gate_x = jax.nn.sigmoid(gz[:, LRU_WIDTH:2 * LRU_WIDTH])
    nl = -lam_ref[...]
    softplus = jnp.maximum(nl, 0.0) + jnp.log(1.0 + jnp.exp(-jnp.abs(nl)))
    log_a = (-LRU_C) * gate_a * softplus
    a = jnp.exp(log_a)
    mult = jnp.sqrt(1.0 - a * a)
    abuf[...] = a
    bbuf[...] = mult * (gate_x * xc)
    return _silu(ga)


def _rglru_scan(hcar, abuf, bbuf):
    def blk(i, h):
        r = pl.multiple_of(i * SUBLANES, SUBLANES)
        a8, b8 = _scan8(abuf[pl.ds(r, SUBLANES), :], bbuf[pl.ds(r, SUBLANES), :])
        rows = a8 * h + b8
        bbuf[pl.ds(r, SUBLANES), :] = rows
        return rows[SUBLANES - 1:SUBLANES, :]

    hcar[...] = lax.fori_loop(0, abuf.shape[0] // SUBLANES, blk, hcar[...])


def _rglru_output(g_silu, bbuf, bng_ref):
    ya = bbuf[...] * g_silu
    ms = jnp.mean(ya * ya, axis=-1, keepdims=True)
    return ya * lax.rsqrt(ms + EPS) * bng_ref[...]


def _mla_prep(lat, ct, st, c128, s128, qg_ref, kvg_ref, wqt_ref, wk_ref, wvt_ref,
              qt_ref, k_ref, vt_ref):
    ts = lat.shape[0]
    q_blk = lat[:, 0:Q_LAT_PAD]
    kv_lat = lat[:, Q_LAT_PAD:Q_LAT_PAD + KV_RANK]
    q_lane = lax.broadcasted_iota(jnp.int32, q_blk.shape, 1)
    q_only = jnp.where(q_lane < Q_RANK, q_blk, 0.0)
    q_ms = jnp.sum(q_only * q_only, axis=-1, keepdims=True) * (1.0 / Q_RANK)
    qn = (q_blk * lax.rsqrt(q_ms + EPS) * qg_ref[...]).astype(BF16)
    kv_ms = jnp.mean(kv_lat * kv_lat, axis=-1, keepdims=True)
    kvn = (kv_lat * lax.rsqrt(kv_ms + EPS) * kvg_ref[...]).astype(BF16)

    qt = lax.dot_general(wqt_ref[...], qn, NT_DIMS, preferred_element_type=F32)
    half = MLA_ROPE // 2
    q_scale = (MLA_QK ** -0.5) * math.log2(math.e)
    pad = jnp.zeros((QK_PAD - MLA_QK, ts), F32)
    for h in range(MLA_HEADS):
        base = h * MLA_QK
        nope = qt[base:base + MLA_NOPE, :]
        pe = qt[base + MLA_NOPE:base + MLA_QK, :]
        pe_sw = jnp.concatenate([pe[half:, :], pe[:half, :]], axis=0)
        pe = pe * ct + pe_sw * st
        qh = jnp.concatenate([nope, pe, pad], axis=0) * q_scale
        qt_ref[0, h] = qh.astype(qt_ref.dtype)

    kn = jnp.dot(kvn, wk_ref[...], preferred_element_type=F32)
    kr = lat[:, LANES:2 * LANES]
    half_swapped = jnp.where(lax.broadcasted_iota(jnp.int32, kr.shape, 1) < MLA_NOPE + half,
                             pltpu.roll(kr, LANES - half, 1), pltpu.roll(kr, half, 1))
    pe128 = kr * c128 + half_swapped * s128
    for h in range(MLA_HEADS):
        k_ref[0, h] = (kn[:, h * LANES:(h + 1) * LANES] + pe128).astype(k_ref.dtype)

    vt = lax.dot_general(wvt_ref[...], kvn, NT_DIMS, preferred_element_type=F32)
    row = lax.broadcasted_iota(jnp.int32, (V_PAD - MLA_V, ts), 0)
    ones_row = jnp.where(row == 0, 1.0, 0.0).astype(F32)
    for h in range(MLA_HEADS):
        vh = jnp.concatenate([vt[h * MLA_V:(h + 1) * MLA_V, :], ones_row], axis=0)
        vt_ref[0, h] = vh.astype(vt_ref.dtype)


def _sgu_branch(c_in, ng_ref, nb_ref, w_ref, bias_ref, bng_ref):
    tm = c_in.shape[0]
    u = _gelu(c_in[:, 0:SGU_WIDTH])
    v = _gelu(c_in[:, SGU_WIDTH:2 * SGU_WIDTH])
    gc = c_in[:, 2 * SGU_WIDTH:3 * SGU_WIDTH]
    mu = jnp.mean(v, axis=-1, keepdims=True)
    vc = v - mu
    var = jnp.mean(vc * vc, axis=-1, keepdims=True)
    vn = (vc * lax.rsqrt(var + EPS) * ng_ref[...] + nb_ref[...]).astype(BF16)

    w = w_ref[...]
    wi = (lax.broadcasted_iota(jnp.int32, w.shape, 0) % SGU_BLOCK) // CHUNK
    wj = lax.broadcasted_iota(jnp.int32, w.shape, 1) // CHUNK
    w = jnp.where(wi >= wj, w, 0.0).astype(BF16)

    lane_group = lax.broadcasted_iota(jnp.int32, (SGU_BLOCK, SGU_WIDTH), 1) // SGU_GROUP_DIM
    mixed = []
    for n in range(tm // SGU_BLOCK):
        r = jnp.dot(w, vn[n * SGU_BLOCK:(n + 1) * SGU_BLOCK, :], preferred_element_type=F32)
        m = r[0:SGU_BLOCK, :]
        for g in range(1, SGU_GROUPS):
            m = jnp.where(lane_group == g, r[g * SGU_BLOCK:(g + 1) * SGU_BLOCK, :], m)
        mixed.append(m + bias_ref[...])
    mixed = jnp.concatenate(mixed, axis=0)

    yc = u * mixed * _silu(gc)
    ms = jnp.mean(yc * yc, axis=-1, keepdims=True)
    return yc * lax.rsqrt(ms + EPS) * bng_ref[...]


def _front_kernel(x_ref, ct_ref, st_ref, c128_ref, s128_ref,
                  preg_ref, wtok_ref, wgbt_ref,
                  convw_ref, convb_ref, wg_ref, bg_ref, lam_ref, bnga_ref,
                  qg_ref, kvg_ref, wqt_ref, wk_ref, wvt_ref,
                  sng_ref, snb_ref, sw_ref, sbias_ref, bngc_ref,
                  ya_ref, qt_ref, k_ref, vt_ref, gbt_ref, yc_ref,
                  xbuf, hcar, abuf, bbuf):
    @pl.when(pl.program_id(1) == 0)
    def _():
        xbuf[0:SUBLANES, :] = jnp.zeros((SUBLANES, LRU_WIDTH), F32)
        hcar[...] = jnp.zeros_like(hcar)

    x = x_ref[0]
    ms = jnp.mean(x * x, axis=-1, keepdims=True)
    h = (x * lax.rsqrt(ms + EPS) * preg_ref[...]).astype(BF16)

    def proj(lo, hi):
        return jnp.dot(h, wtok_ref[:, lo:hi], preferred_element_type=F32)

    g_silu = _rglru_inputs(proj(COL_A, COL_LAT), convw_ref, convb_ref, wg_ref, bg_ref, lam_ref,
                           xbuf, abuf, bbuf)
    lat = proj(COL_LAT, COL_C)
    yc = _sgu_branch(proj(COL_C, N_TOK), sng_ref, snb_ref, sw_ref, sbias_ref, bngc_ref)
    yc_ref[0] = yc.astype(yc_ref.dtype)
    gbt = lax.dot_general(wgbt_ref[...], h, NT_DIMS, preferred_element_type=F32)
    gbt_ref[0] = gbt.astype(gbt_ref.dtype)

    _rglru_scan(hcar, abuf, bbuf)

    ya_ref[0] = _rglru_output(g_silu, bbuf, bnga_ref).astype(ya_ref.dtype)
    _mla_prep(lat, ct_ref[0], st_ref[0], c128_ref[0], s128_ref[0], qg_ref, kvg_ref,
              wqt_ref, wk_ref, wvt_ref, qt_ref, k_ref, vt_ref)


def _layer_spec(a, layer):
    return pl.BlockSpec((None,) + a.shape[1:], lambda b, i: (layer,) + (0,) * (a.ndim - 1))


def _front(x, ct, st, c128, s128, p, layer, tm):
    B, S, D = x.shape
    weights = [p[n] for n in (
        'pre_g', 'w_tok', 'w_gbt',
        'conv_w', 'conv_b', 'w_gate', 'b_gate', 'lam', 'bng_a',
        'q_g', 'kv_g', 'wqt', 'wk', 'wvt',
        'sgu_g', 'sgu_nb', 'w_stack', 'sgu_bias', 'bng_c')]
    full = lambda a: _layer_spec(a, layer)
    return pl.pallas_call(
        _front_kernel,
        grid=(B, S // tm),
        in_specs=[
            pl.BlockSpec((1, tm, D), lambda b, i: (b, i, 0)),
            pl.BlockSpec((1, MLA_ROPE, tm), lambda b, i: (b, 0, i)),
            pl.BlockSpec((1, MLA_ROPE, tm), lambda b, i: (b, 0, i)),
            pl.BlockSpec((1, tm, LANES), lambda b, i: (b, i, 0)),
            pl.BlockSpec((1, tm, LANES), lambda b, i: (b, i, 0)),
        ] + [full(w) for w in weights],
        out_specs=[
            pl.BlockSpec((1, tm, LRU_WIDTH), lambda b, i: (b, i, 0)),
            pl.BlockSpec((1, MLA_HEADS, QK_PAD, tm), lambda b, i: (b, 0, 0, i)),
            pl.BlockSpec((1, MLA_HEADS, tm, QK_PAD), lambda b, i: (b, 0, i, 0)),
            pl.BlockSpec((1, MLA_HEADS, V_PAD, tm), lambda b, i: (b, 0, 0, i)),
            pl.BlockSpec((1, MLA_WIDTH, tm), lambda b, i: (b, 0, i)),
            pl.BlockSpec((1, tm, SGU_WIDTH), lambda b, i: (b, i, 0)),
        ],
        out_shape=[
            jax.ShapeDtypeStruct((B, S, LRU_WIDTH), BF16),
            jax.ShapeDtypeStruct((B, MLA_HEADS, QK_PAD, S), BF16),
            jax.ShapeDtypeStruct((B, MLA_HEADS, S, QK_PAD), BF16),
            jax.ShapeDtypeStruct((B, MLA_HEADS, V_PAD, S), BF16),
            jax.ShapeDtypeStruct((B, MLA_WIDTH, S), BF16),
            jax.ShapeDtypeStruct((B, S, SGU_WIDTH), BF16),
        ],
        scratch_shapes=[
            pltpu.VMEM((tm + SUBLANES, LRU_WIDTH), F32),
            pltpu.VMEM((1, LRU_WIDTH), F32),
            pltpu.VMEM((tm, LRU_WIDTH), F32),
            pltpu.VMEM((tm, LRU_WIDTH), F32),
        ],
        compiler_params=_cparams("parallel", "arbitrary"),
        name="front",
    )(x, ct, st, c128, s128, *weights)


def _attn_kernel(qt_ref, k_ref, vt_ref, o_ref, s_0, s_1, s_2, s_3, acc_scr, m_scr):
    s_bufs = (s_0, s_1, s_2, s_3)
    S = k_ref.shape[2]
    tk, tq = s_0.shape
    nq = S // tq
    n_off = nq * (nq - 1) // 2
    assert tk == tq and ATT_KBIG % len(s_bufs) == 0
    assert n_off % ATT_KBIG == 0 and nq % ATT_KBIG == 0

    m_scr[...] = jnp.full(m_scr.shape, -jnp.inf, F32)
    acc_scr[...] = jnp.zeros_like(acc_scr)

    def fold(x):
        return x.reshape(x.shape[0] // SUBLANES, SUBLANES, tq)

    def score(j, i, s_buf, diagonal):
        jc = jnp.minimum(j, nq - 1)
        ic = jnp.minimum(i, nq - 1)
        ks = pl.multiple_of(jc * tk, tk)
        qs = pl.multiple_of(ic * tq, tq)
        s = jnp.dot(k_ref[0, 0, pl.ds(ks, tk), :], qt_ref[0, 0, :, pl.ds(qs, tq)],
                    preferred_element_type=F32)
        if diagonal:
            kc = lax.broadcasted_iota(jnp.int32, s.shape, 0) // CHUNK
            qc = lax.broadcasted_iota(jnp.int32, s.shape, 1) // CHUNK
            s = jnp.where(kc <= qc, s, -jnp.inf)
        s_buf[...] = s
        return jnp.max(jnp.max(fold(s), axis=0), axis=0, keepdims=True)

    def finish(j, i, s_buf, mx):
        ks = pl.multiple_of(j * tk, tk)
        qs = pl.multiple_of(i * tq, tq)
        m_old = m_scr[:, pl.ds(qs, tq)]
        m_new = jnp.maximum(m_old, mx)
        alpha = jnp.exp2(m_old - m_new)
        p = jnp.exp2(s_buf[...] - m_new).astype(BF16)
        pv = jnp.dot(vt_ref[0, 0, :, pl.ds(ks, tk)], p, preferred_element_type=F32)
        acc_scr[:, pl.ds(qs, tq)] = alpha * acc_scr[:, pl.ds(qs, tq)] + pv
        m_scr[:, pl.ds(qs, tq)] = m_new

    def run(n_tiles, first, advance, diagonal):
        nbuf = len(s_bufs)

        def body(_, carry):
            (j0, i0, mx0), (j1, i1, mx1) = carry
            for u in range(ATT_KBIG):
                j2, i2 = advance(j1, i1)
                mx2 = score(j2, i2, s_bufs[(u + 2) % nbuf], diagonal)
                finish(j0, i0, s_bufs[u % nbuf], mx0)
                (j0, i0, mx0), (j1, i1, mx1) = (j1, i1, mx1), (j2, i2, mx2)
            return (j0, i0, mx0), (j1, i1, mx1)

        j0, i0 = first
        j1, i1 = advance(j0, i0)
        t0 = (j0, i0, score(j0, i0, s_bufs[0], diagonal))
        t1 = (j1, i1, score(j1, i1, s_bufs[1], diagonal))
        lax.fori_loop(0, n_tiles // ATT_KBIG, body, (t0, t1))

    def next_off_diagonal(j, i):
        wrap = i == nq - 1
        return jnp.where(wrap, j + 1, j), jnp.where(wrap, j + 2, i + 1)

    run(n_off, (jnp.int32(0), jnp.int32(1)), next_off_diagonal, False)
    run(nq, (jnp.int32(0), jnp.int32(0)), lambda j, i: (j + 1, i + 1), True)

    l = acc_scr[MLA_V:MLA_V + 1, :]
    o_ref[0, 0] = (acc_scr[0:MLA_V, :] / l).astype(o_ref.dtype)


def _attention(qt, k, vt, tq):
    B, H, _, S = qt.shape
    return pl.pallas_call(
        _attn_kernel,
        grid=(B, H),
        in_specs=[
            pl.BlockSpec((1, 1, QK_PAD, S), lambda b, h: (b, h, 0, 0)),
            pl.BlockSpec((1, 1, S, QK_PAD), lambda b, h: (b, h, 0, 0)),
            pl.BlockSpec((1, 1, V_PAD, S), lambda b, h: (b, h, 0, 0)),
        ],
        out_specs=pl.BlockSpec((1, 1, MLA_V, S), lambda b, h: (b, h, 0, 0)),
        out_shape=jax.ShapeDtypeStruct((B, H, MLA_V, S), BF16),
        scratch_shapes=[pltpu.VMEM((tq, tq), F32)] * 4 + [
                        pltpu.VMEM((V_PAD, S), F32), pltpu.VMEM((1, S), F32)],
        compiler_params=_cparams("parallel", "parallel"),
        name="attention",
    )(qt, k, vt)


def _outproj_kernel(x_ref, ya_ref, ot_ref, gbt_ref, yc_ref, bngb_ref, w_ref, pg_ref, out_ref):
    gbt = gbt_ref[0].astype(F32)
    ybt = ot_ref[0].astype(F32) * _silu(gbt)
    ms = jnp.mean(ybt * ybt, axis=0, keepdims=True)
    yb = (ybt * lax.rsqrt(ms + EPS) * bngb_ref[...]).T.astype(BF16)
    y = jnp.dot(jnp.concatenate([ya_ref[0], yb, yc_ref[0]], axis=1), w_ref[...],
                preferred_element_type=F32)
    ms = jnp.mean(y * y, axis=-1, keepdims=True)
    out_ref[0] = x_ref[0] + y * lax.rsqrt(ms + EPS) * pg_ref[...]


def _outproj(x, ya, ot, gbt, yc, p, layer, tm):
    B, S, D = x.shape
    bng_b, w_out, post_g = p['bng_b'], p['w_out'], p['post_g']
    return pl.pallas_call(
        _outproj_kernel,
        grid=(B, S // tm),
        in_specs=[
            pl.BlockSpec((1, tm, D), lambda b, i: (b, i, 0)),
            pl.BlockSpec((1, tm, LRU_WIDTH), lambda b, i: (b, i, 0)),
            pl.BlockSpec((1, MLA_WIDTH, tm), lambda b, i: (b, 0, i)),
            pl.BlockSpec((1, MLA_WIDTH, tm), lambda b, i: (b, 0, i)),
            pl.BlockSpec((1, tm, SGU_WIDTH), lambda b, i: (b, i, 0)),
            _layer_spec(bng_b, layer),
            _layer_spec(w_out, layer),
            _layer_spec(post_g, layer),
        ],
        out_specs=pl.BlockSpec((1, tm, D), lambda b, i: (b, i, 0)),
        out_shape=jax.ShapeDtypeStruct((B, S, D), F32),
        compiler_params=_cparams("parallel", "parallel"),
        name="outproj",
    )(x, ya, ot, gbt, yc, bng_b, w_out, post_g)


def _stacked_params(pre_norm_g, w_in, conv_w, conv_b, lru_wa, lru_ba, lru_wx, lru_bx, lru_lambda,
                    q_norm_g, w_uq, kv_norm_g, w_ukv, sgu_norm_g, sgu_norm_b, sgu_w, sgu_b,
                    branch_norm_g, w_out, post_norm_g):
    L, D, _ = w_in.shape
    offs = np.cumsum([0, LRU_WIDTH, LRU_WIDTH, Q_RANK, KV_RANK, MLA_ROPE, MLA_WIDTH,
                      SGU_WIDTH, SGU_WIDTH, SGU_WIDTH])
    cols = lambda i: w_in[:, :, offs[i]:offs[i + 1]]
    row = lambda v: v[:, None, :]
    w_tok = jnp.concatenate([
        cols(0), cols(1),
        cols(2), cols(4), jnp.zeros((L, D, Q_LAT_PAD - Q_RANK - MLA_ROPE), F32), cols(3),
        cols(6), cols(7), cols(8),
    ], axis=2).astype(BF16)
    w_gbt = jnp.swapaxes(cols(5), 1, 2).astype(BF16)

    def block_diag(w):
        eye = jnp.eye(LRU_HEADS, dtype=w.dtype)
        return jnp.einsum('lhij,hg->lhigj', w, eye).reshape(L, LRU_WIDTH, LRU_WIDTH)

    w_gate = jnp.concatenate([block_diag(lru_wa), block_diag(lru_wx)], axis=2).astype(BF16)
    b_gate = row(jnp.concatenate([lru_ba, lru_bx], axis=1))

    q_pad = Q_LAT_PAD - Q_RANK
    q_g = row(jnp.concatenate([q_norm_g, jnp.zeros((L, q_pad), F32)], axis=1))
    wqt = jnp.concatenate([w_uq, jnp.zeros((L, q_pad, w_uq.shape[2]), F32)], axis=1)
    wqt = jnp.swapaxes(wqt, 1, 2).astype(BF16)
    wkv = w_ukv.reshape(L, KV_RANK, MLA_HEADS, MLA_NOPE + MLA_V)
    wk = jnp.concatenate([wkv[..., :MLA_NOPE],
                          jnp.zeros((L, KV_RANK, MLA_HEADS, LANES - MLA_NOPE), F32)], axis=3)
    wk = wk.reshape(L, KV_RANK, MLA_HEADS * LANES).astype(BF16)
    wvt = wkv[..., MLA_NOPE:].reshape(L, KV_RANK, MLA_HEADS * MLA_V)
    wvt = jnp.swapaxes(wvt, 1, 2).astype(BF16)

    w_stack = sgu_w.reshape(L, SGU_GROUPS * SGU_BLOCK, SGU_BLOCK)
    sgu_bias = jnp.repeat(jnp.swapaxes(sgu_b, 1, 2), SGU_GROUP_DIM, axis=2)

    bng = branch_norm_g
    return dict(
        pre_g=row(pre_norm_g), w_tok=w_tok, w_gbt=w_gbt,
        conv_w=conv_w, conv_b=row(conv_b), w_gate=w_gate, b_gate=b_gate, lam=row(lru_lambda),
        q_g=q_g, kv_g=row(kv_norm_g), wqt=wqt, wk=wk, wvt=wvt,
        sgu_g=row(sgu_norm_g), sgu_nb=row(sgu_norm_b), w_stack=w_stack, sgu_bias=sgu_bias,
        bng_a=row(bng[:, :LRU_WIDTH]),
        bng_b=bng[:, LRU_WIDTH:LRU_WIDTH + MLA_WIDTH, None],
        bng_c=row(bng[:, LRU_WIDTH + MLA_WIDTH:]),
        w_out=w_out.astype(BF16), post_g=row(post_norm_g),
    )


def kernel(x, positions, pre_norm_g, w_in, conv_w, conv_b, lru_wa, lru_ba, lru_wx, lru_bx, lru_lambda,
           q_norm_g, w_uq, kv_norm_g, w_ukv, sgu_norm_g, sgu_norm_b, sgu_w, sgu_b, branch_norm_g,
           w_out, post_norm_g):
    B, S, D = x.shape
    depth = w_in.shape[0]
    tm = min(ROW_TILE, S)
    tq = min(ATT_TQ, S)
    to = min(OUT_TILE, S)
    assert S % tm == 0 and S % tq == 0 and S % to == 0 and tq % CHUNK == 0 and tm % SGU_BLOCK == 0

    ct, st, c128, s128 = _rope_tables(positions, tm)
    p = _stacked_params(pre_norm_g, w_in, conv_w, conv_b, lru_wa, lru_ba, lru_wx, lru_bx,
                        lru_lambda, q_norm_g, w_uq, kv_norm_g, w_ukv, sgu_norm_g, sgu_norm_b,
                        sgu_w, sgu_b, branch_norm_g, w_out, post_norm_g)
    h = x
    for layer in range(depth):
        ya, qt, k, vt, gbt, yc = _front(h, ct, st, c128, s128, p, layer, tm)
        ot = _attention(qt, k, vt, tq).reshape(B, MLA_WIDTH, S)
        h = _outproj(h, ya, ot, gbt, yc, p, layer, to)
    return h
```
